```python
import jax, jax.numpy as jnp
from jax import lax
import numpy as np

D_MODEL = 4096
BATCH = 4
SEQ = 2048
DEPTH = 2
DEC_BATCH = 8
DEC_SEQ = 8
PAST_LEN = 16384
PAGE_SIZE = 128

N_META = 16
N_A_LAYERS = DEPTH // 2
N_B_LAYERS = DEPTH - N_A_LAYERS
CONV_WIDTH = 3
D_CONV = D_MODEL
HEAD_DIM = 128
N_HEADS = D_MODEL // HEAD_DIM
D_ATTN = N_HEADS * HEAD_DIM
QUERY_BLOCK = 128
RMS_EPS = 1e-6

kernel_name = "yoco_shortconv_forgetting_attention_step"


def rms_norm(x, g):
    xf = x.astype(jnp.float32)
    y = xf * lax.rsqrt(jnp.mean(xf * xf, axis=-1, keepdims=True) + RMS_EPS)
    return (y * g.astype(jnp.float32)).astype(x.dtype)


def causal_conv(u, ctx, w):
    up = jnp.concatenate([ctx.astype(u.dtype), u], axis=1)
    L = u.shape[1]
    y = w[0] * up[:, 0:L]
    for k in range(1, CONV_WIDTH):
        y = y + w[k] * up[:, k:k + L]
    new_ctx = up[:, up.shape[1] - (CONV_WIDTH - 1):]
    return y, new_ctx


def conv_mixer(x, ctx, w_in, conv_w, w_out):
    proj = x @ w_in
    b_gate, c_gate, h, z = jnp.split(proj, 4, axis=-1)
    y, new_ctx = causal_conv(c_gate * h, ctx, conv_w)
    return (b_gate * y * jax.nn.silu(z)) @ w_out, new_ctx


def shared_kv(h, kv_g, w_kvf, b_f):
    Bn, L, _ = h.shape
    hn = rms_norm(h, kv_g)
    proj = hn @ w_kvf
    k = proj[..., :D_ATTN].reshape(Bn, L, N_HEADS, HEAD_DIM)
    v = proj[..., D_ATTN:2 * D_ATTN].reshape(Bn, L, N_HEADS, HEAD_DIM)
    logf = jax.nn.log_sigmoid(proj[..., 2 * D_ATTN:].astype(jnp.float32) + b_f.astype(jnp.float32))
    return k, v, logf


def forgetting_attention(q, f_q, q_pos, segments):
    Bn, Lq, H, Dh = q.shape
    qb = QUERY_BLOCK if Lq % QUERY_BLOCK == 0 else Lq
    nb = Lq // qb
    scale = HEAD_DIM ** -0.5
    q_blocks = q.reshape(Bn, nb, qb, H, Dh).transpose(1, 0, 2, 3, 4)
    f_blocks = f_q.reshape(Bn, nb, qb, H).transpose(1, 0, 3, 2)
    p_blocks = q_pos.reshape(nb, qb)
    seg_f = [fk.astype(jnp.float32).transpose(0, 2, 1) for (_, _, fk, _) in segments]

    def one_block(args):
        q_b, fq_b, pos_b = args
        logits = []
        for (k, _, _, k_pos), fk in zip(segments, seg_f):
            s = jnp.einsum('bqhd,bkhd->bhqk', q_b, k).astype(jnp.float32) * scale
            s = s + fq_b[..., :, None] - fk[:, :, None, :]
            s = jnp.where(k_pos[None, :] <= pos_b[:, None], s, -jnp.inf)
            logits.append(s)
        p = jax.nn.softmax(jnp.concatenate(logits, axis=-1), axis=-1)
        out = None
        off = 0
        for (_, v, _, _) in segments:
            lk = v.shape[1]
            o = jnp.einsum('bhqk,bkhd->bqhd', p[..., off:off + lk].astype(v.dtype), v)
            out = o if out is None else out + o
            off += lk
        return out

    out = lax.map(one_block, (q_blocks, f_blocks, p_blocks))
    return out.transpose(1, 0, 2, 3, 4).reshape(Bn, Lq, H, Dh)


def fox_mixer(x, segments, f_q, q_pos, w_qz, w_o):
    Bn, L, _ = x.shape
    proj = x @ w_qz
    q = proj[..., :D_ATTN].reshape(Bn, L, N_HEADS, HEAD_DIM)
    z = proj[..., D_ATTN:]
    o = forgetting_attention(q, f_q, q_pos, segments).reshape(Bn, L, D_ATTN)
    return (o * jax.nn.silu(z)) @ w_o


def setup_inputs(seed: int = 0) -> dict:
    key = jax.random.key(seed)
    ks = jax.random.split(key, 24)
    f32 = jnp.float32
    n_pages = PAST_LEN // PAGE_SIZE
    n_used = DEC_BATCH * n_pages
    n_pool = n_used + max(1, n_used // 4)
    sd = D_MODEL ** -0.5
    page_table = jax.random.permutation(ks[0], n_pool)[:n_used].reshape(DEC_BATCH, n_pages).astype(jnp.int32)
    return {
        "x_prompt": jax.random.normal(ks[1], (BATCH, SEQ, D_MODEL), f32),
        "x_sample": jax.random.normal(ks[2], (DEC_BATCH, DEC_SEQ, D_MODEL), f32),
        "state_conv": jax.random.normal(ks[3], (N_A_LAYERS, DEC_BATCH, CONV_WIDTH - 1, D_CONV), f32),
        "cache_k": jax.random.normal(ks[4], (n_pool, PAGE_SIZE, N_HEADS, HEAD_DIM), f32),
        "cache_v": jax.random.normal(ks[5], (n_pool, PAGE_SIZE, N_HEADS, HEAD_DIM), f32),
        "cache_logf": jax.nn.log_sigmoid(4.0 + jax.random.normal(ks[6], (n_pool, PAGE_SIZE, N_HEADS), f32)),
        "page_table": page_table,
        "meta_tokens": jax.random.normal(ks[7], (N_META, D_MODEL), f32),
        "a_pre_g": 1.0 + 0.02 * jax.random.normal(ks[8], (N_A_LAYERS, D_MODEL), f32),
        "a_post_g": 1.0 + 0.02 * jax.random.normal(ks[9], (N_A_LAYERS, D_MODEL), f32),
        "a_w_in": sd * jax.random.normal(ks[10], (N_A_LAYERS, D_MODEL, 4 * D_CONV), f32),
        "a_conv_w": (CONV_WIDTH ** -0.5) * jax.random.normal(ks[11], (N_A_LAYERS, CONV_WIDTH, D_CONV), f32),
        "a_w_out": (D_CONV ** -0.5) * jax.random.normal(ks[12], (N_A_LAYERS, D_CONV, D_MODEL), f32),
        "kv_g": 1.0 + 0.02 * jax.random.normal(ks[13], (D_MODEL,), f32),
        "w_kvf": sd * jax.random.normal(ks[14], (D_MODEL, 2 * D_ATTN + N_HEADS), f32),
        "b_f": jax.random.uniform(ks[15], (N_HEADS,), f32, 2.0, 6.0),
        "b_pre_g": 1.0 + 0.02 * jax.random.normal(ks[16], (N_B_LAYERS, D_MODEL), f32),
        "b_post_g": 1.0 + 0.02 * jax.random.normal(ks[17], (N_B_LAYERS, D_MODEL), f32),
        "b_w_qz": sd * jax.random.normal(ks[18], (N_B_LAYERS, D_MODEL, 2 * D_ATTN), f32),
        "b_w_o": (D_ATTN ** -0.5) * jax.random.normal(ks[19], (N_B_LAYERS, D_ATTN, D_MODEL), f32),
    }


def reference(x_prompt, x_sample, state_conv, cache_k, cache_v, cache_logf, page_table,
              meta_tokens, a_pre_g, a_post_g, a_w_in, a_conv_w, a_w_out,
              kv_g, w_kvf, b_f, b_pre_g, b_post_g, b_w_qz, b_w_o):
    Bp = x_prompt.shape[0]
    Bs = x_sample.shape[0]
    n_pages = PAST_LEN // PAGE_SIZE
    meta = jnp.broadcast_to(meta_tokens[None].astype(x_prompt.dtype), (Bp, N_META, D_MODEL))
    hp = jnp.concatenate([meta, x_prompt], axis=1)
    hs = x_sample
    conv_p, conv_s = [], []
    for layer in range(DEPTH):
        if layer < N_A_LAYERS:
            i = layer
            ctx0 = jnp.zeros((Bp, CONV_WIDTH - 1, D_CONV), hp.dtype)
            a_p, st_p = conv_mixer(rms_norm(hp, a_pre_g[i]), ctx0, a_w_in[i], a_conv_w[i], a_w_out[i])
            hp = hp + rms_norm(a_p, a_post_g[i])
            a_s, st_s = conv_mixer(rms_norm(hs, a_pre_g[i]), state_conv[i], a_w_in[i], a_conv_w[i], a_w_out[i])
            hs = hs + rms_norm(a_s, a_post_g[i])
            conv_p.append(st_p)
            conv_s.append(st_s)
            if layer == N_A_LAYERS - 1:
                k_prompt, v_prompt, logf_prompt = shared_kv(hp, kv_g, w_kvf, b_f)
                F_p = jnp.cumsum(logf_prompt, axis=1)
                seg_p = [(k_prompt, v_prompt, F_p, jnp.arange(N_META + SEQ))]
                fq_p = F_p[:, N_META:]
                qpos_p = N_META + jnp.arange(SEQ)
                k_sample, v_sample, logf_sample = shared_kv(hs, kv_g, w_kvf, b_f)
                k_past = cache_k[page_table].reshape(Bs, n_pages * PAGE_SIZE, N_HEADS, HEAD_DIM)
                v_past = cache_v[page_table].reshape(Bs, n_pages * PAGE_SIZE, N_HEADS, HEAD_DIM)
                lf_past = cache_logf[page_table].reshape(Bs, n_pages * PAGE_SIZE, N_HEADS).astype(jnp.float32)
                F_all = jnp.cumsum(jnp.concatenate([lf_past, logf_sample], axis=1), axis=1)
                F_past = F_all[:, :PAST_LEN]
                F_new = F_all[:, PAST_LEN:]
                qpos_s = PAST_LEN + jnp.arange(DEC_SEQ)
                seg_s = [(k_past.astype(hs.dtype), v_past.astype(hs.dtype), F_past, jnp.arange(PAST_LEN)),
                         (k_sample, v_sample, F_new, qpos_s)]
                hp = hp[:, N_META:]
        else:
            j = layer - N_A_LAYERS
            b_p = fox_mixer(rms_norm(hp, b_pre_g[j]), seg_p, fq_p, qpos_p, b_w_qz[j], b_w_o[j])
            hp = hp + rms_norm(b_p, b_post_g[j])
            b_s = fox_mixer(rms_norm(hs, b_pre_g[j]), seg_s, F_new, qpos_s, b_w_qz[j], b_w_o[j])
            hs = hs + rms_norm(b_s, b_post_g[j])
    y_prompt = hp
    y_sample = hs
    state_conv_prompt = jnp.stack(conv_p, axis=0)
    state_conv_sample = jnp.stack(conv_s, axis=0)
    return (y_prompt, y_sample, state_conv_prompt, state_conv_sample,
            k_prompt, v_prompt, logf_prompt, k_sample, v_sample, logf_sample)
```

```python
import functools

import jax
import jax.numpy as jnp
from jax import lax
from jax.experimental import pallas as pl
from jax.experimental.pallas import tpu as pltpu

F32 = jnp.float32
BF16 = jnp.bfloat16

RMS_EPS = 1e-6
CONV_WIDTH = 3
HEAD_DIM = 128
LANES = 128
CARRY_ROWS = 8
MASKED = -1e30
VMEM_LIMIT_BYTES = 60 * 1024 * 1024


def _params(*sem):
    return pltpu.CompilerParams(dimension_semantics=sem, vmem_limit_bytes=VMEM_LIMIT_BYTES)


def _dot(a, b):
    return jnp.dot(a, b, preferred_element_type=F32)


def _dot_nt(a, b):
    return lax.dot_general(a, b, (((1,), (1,)), ((), ())), preferred_element_type=F32)


def _split3(x):
    hi = x.astype(BF16)
    r1 = x - hi.astype(F32)
    mid = r1.astype(BF16)
    lo = (r1 - mid.astype(F32)).astype(BF16)
    return hi, mid, lo


def _sigmoid(z):
    return 1.0 / (1.0 + jnp.exp(-z))


def _row_tile(rows, want):
    return want if rows % want == 0 else rows


def _norm_kernel(x_ref, g_ref, *o_refs):
    x = x_ref[...]
    xhat = x * lax.rsqrt(jnp.mean(x * x, axis=-1, keepdims=True) + RMS_EPS)
    for n, o_ref in enumerate(o_refs):
        o_ref[...] = (xhat * g_ref[n:n + 1, :]).astype(o_ref.dtype)


def rms_norm_bf16(x, gains):
    rows, d = x.shape
    n = gains.shape[0]
    tm = _row_tile(rows, 512)
    return pl.pallas_call(
        _norm_kernel,
        grid=(rows // tm,),
        in_specs=[pl.BlockSpec((tm, d), lambda i: (i, 0)),
                  pl.BlockSpec((n, d), lambda i: (0, 0))],
        out_specs=[pl.BlockSpec((tm, d), lambda i: (i, 0))] * n,
        out_shape=[jax.ShapeDtypeStruct((rows, d), BF16)] * n,
        compiler_params=_params("parallel"),
        name="rms_norm",
    )(x, gains)


def _gated_conv(x, wb_ref, wc_ref, wh_ref, wz_ref, cw_ref, um1_fix, um2_fix):
    u = _dot(x, wc_ref[...]) * _dot(x, wh_ref[...])
    um1 = um1_fix(pltpu.roll(u, 1, 0))
    um2 = um2_fix(pltpu.roll(u, 2, 0))
    cw = cw_ref[...]
    y = cw[0:1, :] * um2 + cw[1:2, :] * um1 + cw[2:3, :] * u
    z = _dot(x, wz_ref[...])
    g = _dot(x, wb_ref[...]) * y * (z * _sigmoid(z))
    return u, g


def _conv_big_kernel(x_ref, wb_ref, wc_ref, wh_ref, wz_ref, cw_ref, ctx_ref,
                     g_ref, tail_ref, carry_ref, *, tiles_per_seq):
    i = pl.program_id(1)

    @pl.when(i % tiles_per_seq == 0)
    def _():
        carry_ref[...] = ctx_ref[...]

    carry = carry_ref[...]
    prev1 = carry[CARRY_ROWS - 1:CARRY_ROWS, :]
    prev2 = carry[CARRY_ROWS - 2:CARRY_ROWS - 1, :]
    tm, tn = g_ref.shape
    row = lax.broadcasted_iota(jnp.int32, (tm, tn), 0)
    u, g = _gated_conv(
        x_ref[...], wb_ref, wc_ref, wh_ref, wz_ref, cw_ref,
        lambda r1: jnp.where(row == 0, prev1, r1),
        lambda r2: jnp.where(row == 0, prev2, jnp.where(row == 1, prev1, r2)))
    tail = u[tm - CARRY_ROWS:, :]
    carry_ref[...] = tail
    tail_ref[0] = tail
    g_ref[...] = g.astype(g_ref.dtype)


def _conv_small_kernel(x_ref, wb_ref, wc_ref, wh_ref, wz_ref, cw_ref,
                       m1_ref, m2_ref, p1_ref, p2_ref, g_ref, u_ref):
    u, g = _gated_conv(
        x_ref[...], wb_ref, wc_ref, wh_ref, wz_ref, cw_ref,
        lambda r1: r1 * m1_ref[...] + p1_ref[...],
        lambda r2: r2 * m2_ref[...] + p2_ref[...])
    u_ref[...] = u
    g_ref[...] = g.astype(g_ref.dtype)


def _w_in_specs(d, dc, tn):
    nj = dc // tn
    return [pl.BlockSpec((d, tn), functools.partial(lambda c, j, i: (0, c * nj + j), c))
            for c in range(4)]


def conv_mixer_big(xn, w_in, conv_w, ctx, seq_len, tm=1024, tn=256):
    rows, d = xn.shape
    dc = w_in.shape[1] // 4
    tm = min(tm, seq_len)
    tps = seq_len // tm
    n_seq = rows // seq_len
    return pl.pallas_call(
        functools.partial(_conv_big_kernel, tiles_per_seq=tps),
        grid=(dc // tn, rows // tm),
        in_specs=[pl.BlockSpec((tm, d), lambda j, i: (i, 0))] + _w_in_specs(d, dc, tn) + [
            pl.BlockSpec((CONV_WIDTH, tn), lambda j, i: (0, j)),
            pl.BlockSpec((CARRY_ROWS, tn), lambda j, i: (0, j))],
        out_specs=[pl.BlockSpec((tm, tn), lambda j, i: (i, j)),
                   pl.BlockSpec((1, CARRY_ROWS, tn), lambda j, i: (i // tps, 0, j))],
        out_shape=[jax.ShapeDtypeStruct((rows, dc), BF16),
                   jax.ShapeDtypeStruct((n_seq, CARRY_ROWS, dc), F32)],
        scratch_shapes=[pltpu.VMEM((CARRY_ROWS, tn), F32)],
        compiler_params=_params("arbitrary", "arbitrary"),
        name="conv_mixer_big",
    )(xn, w_in, w_in, w_in, w_in, conv_w, ctx)


def conv_mixer_small(xn, w_in, conv_w, m1, m2, p1, p2, tn=256):
    rows, d = xn.shape
    dc = w_in.shape[1] // 4
    return pl.pallas_call(
        _conv_small_kernel,
        grid=(dc // tn, 1),
        in_specs=[pl.BlockSpec((rows, d), lambda j, i: (0, 0))] + _w_in_specs(d, dc, tn) + [
            pl.BlockSpec((CONV_WIDTH, tn), lambda j, i: (0, j)),
            pl.BlockSpec((rows, 1), lambda j, i: (0, 0)),
            pl.BlockSpec((rows, 1), lambda j, i: (0, 0)),
            pl.BlockSpec((rows, tn), lambda j, i: (0, j)),
            pl.BlockSpec((rows, tn), lambda j, i: (0, j))],
        out_specs=[pl.BlockSpec((rows, tn), lambda j, i: (0, j)),
                   pl.BlockSpec((rows, tn), lambda j, i: (0, j))],
        out_shape=[jax.ShapeDtypeStruct((rows, dc), BF16),
                   jax.ShapeDtypeStruct((rows, dc), F32)],
        compiler_params=_params("parallel", "arbitrary"),
        name="conv_mixer_small",
    )(xn, w_in, w_in, w_in, w_in, conv_w, m1, m2, p1, p2)


def _proj_residual_kernel(x_ref, w_ref, h_ref, g_ref, o_ref, *, nk):
    k = pl.program_id(1)
    part = _dot(x_ref[...], w_ref[...])

    @pl.when(k == 0)
    def _():
        o_ref[...] = part

    @pl.when(k > 0)
    def _():
        o_ref[...] += part

    @pl.when(k == nk - 1)
    def _():
        a = o_ref[...]
        inv = lax.rsqrt(jnp.mean(a * a, axis=-1, keepdims=True) + RMS_EPS)
        o_ref[...] = h_ref[...] + a * inv * g_ref[...]


def proj_residual(x, w, h, gain, tm=512, tk=512):
    rows, kdim = x.shape
    d = w.shape[1]
    tm = _row_tile(rows, tm)
    nk = kdim // tk
    return pl.pallas_call(
        functools.partial(_proj_residual_kernel, nk=nk),
        grid=(rows // tm, nk),
        in_specs=[pl.BlockSpec((tm, tk), lambda i, k: (i, k)),
                  pl.BlockSpec((tk, d), lambda i, k: (k, 0)),
                  pl.BlockSpec((tm, d), lambda i, k: (i, 0)),
                  pl.BlockSpec((1, d), lambda i, k: (0, 0))],
        out_specs=pl.BlockSpec((tm, d), lambda i, k: (i, 0)),
        out_shape=jax.ShapeDtypeStruct((rows, d), F32),
        compiler_params=_params("parallel", "arbitrary"),
        name="proj_residual",
    )(x, w, h, gain)


def _mm_kernel(x_ref, w_ref, o_ref, *, epilogue):
    o_ref[...] = epilogue(_dot(x_ref[...], w_ref[...])).astype(o_ref.dtype)


def matmul(x, w, out_dtype, epilogue=lambda y: y, tm=1024, tn=1024):
    rows, kdim = x.shape
    n = w.shape[1]
    tm = _row_tile(rows, tm)
    tn = min(tn, n)
    return pl.pallas_call(
        functools.partial(_mm_kernel, epilogue=epilogue),
        grid=(n // tn, rows // tm),
        in_specs=[pl.BlockSpec((tm, kdim), lambda j, i: (i, 0)),
                  pl.BlockSpec((kdim, tn), lambda j, i: (0, j))],
        out_specs=pl.BlockSpec((tm, tn), lambda j, i: (i, j)),
        out_shape=jax.ShapeDtypeStruct((rows, n), out_dtype),
        compiler_params=_params("parallel", "parallel"),
        name="matmul",
    )(x, w)


def _silu(z):
    return z * _sigmoid(z)


def _forget_gate_kernel(x_ref, w_ref, b_ref, o_ref):
    t = _dot(x_ref[...], w_ref[...]) + b_ref[...]
    o_ref[...] = jnp.minimum(t, 0.0) - jnp.log(1.0 + jnp.exp(-jnp.abs(t)))


def forget_gate(x, w_f, b_f):
    rows, kdim = x.shape
    tm = _row_tile(rows, 1024)
    return pl.pallas_call(
        _forget_gate_kernel,
        grid=(rows // tm,),
        in_specs=[pl.BlockSpec((tm, kdim), lambda i: (i, 0)),
                  pl.BlockSpec((kdim, LANES), lambda i: (0, 0)),
                  pl.BlockSpec((1, LANES), lambda i: (0, 0))],
        out_specs=pl.BlockSpec((tm, LANES), lambda i: (i, 0)),
        out_shape=jax.ShapeDtypeStruct((rows, LANES), F32),
        compiler_params=_params("parallel"),
        name="forget_gate",
    )(x, w_f, b_f)


def _cumsum_kernel(x_ref, o_ref, *, n_chunks):
    r = lax.broadcasted_iota(jnp.int32, (LANES, LANES), 0)
    c = lax.broadcasted_iota(jnp.int32, (LANES, LANES), 1)
    tri = jnp.where(c <= r, 1.0, 0.0).astype(BF16)
    carry = jnp.zeros((1, LANES), F32)
    for n in range(n_chunks):
        hi, mid, lo = _split3(x_ref[0, n * LANES:(n + 1) * LANES, :])
        f = _dot(tri, hi) + _dot(tri, mid) + _dot(tri, lo) + carry
        o_ref[0, n * LANES:(n + 1) * LANES, :] = f
        carry = f[LANES - 1:LANES, :]


def cumsum_rows(x):
    b, length, lanes = x.shape
    return pl.pallas_call(
        functools.partial(_cumsum_kernel, n_chunks=length // LANES),
        grid=(b,),
        in_specs=[pl.BlockSpec((1, length, lanes), lambda i: (i, 0, 0))],
        out_specs=pl.BlockSpec((1, length, lanes), lambda i: (i, 0, 0)),
        out_shape=jax.ShapeDtypeStruct(x.shape, F32),
        compiler_params=_params("parallel"),
        name="cumsum_rows",
    )(x)


def _attn_prompt_kernel(q_ref, k_ref, v_ref, km_ref, vm_ref, fq_ref, fk_ref, fkm_ref,
                        sz_ref, o_ref, kb_ref, vb_ref, *, heads_per_step, n_meta, tq):
    hg = pl.program_id(1)
    lq = q_ref.shape[0]
    nq = lq // tq
    n_heads = fq_ref.shape[2]
    pad = LANES - n_meta
    head_lane = lax.broadcasted_iota(jnp.int32, (tq, n_heads), 1)
    col = lax.broadcasted_iota(jnp.int32, (tq, tq), 1)
    row = lax.broadcasted_iota(jnp.int32, (tq, tq), 0)
    meta_lane = lax.broadcasted_iota(jnp.int32, (tq, LANES), 1)

    for hh in range(heads_per_step):
        sl = slice(hh * HEAD_DIM, (hh + 1) * HEAD_DIM)
        head = hg * heads_per_step + hh
        kb_ref[...] = k_ref[:, sl].astype(BF16)
        vb_ref[...] = v_ref[:, sl].astype(BF16)
        zeros = jnp.zeros((pad, HEAD_DIM), F32)
        k_meta = jnp.concatenate([zeros, km_ref[:, sl]], axis=0).astype(BF16)
        v_meta = jnp.concatenate([zeros, vm_ref[:, sl]], axis=0).astype(BF16)
        f_meta = fkm_ref[0, hh]

        def q_block(qi, _, sl=sl, hh=hh, head=head, k_meta=k_meta, v_meta=v_meta,
                    f_meta=f_meta):
            qs = pl.multiple_of(qi * tq, tq)
            q = q_ref[pl.ds(qs, tq), sl]
            f_q = jnp.sum(jnp.where(head_lane == head, fq_ref[0, pl.ds(qs, tq), :], 0.0),
                          axis=-1, keepdims=True)

            s = _dot_nt(q, k_meta) + (f_q - f_meta)
            s = jnp.where(meta_lane >= pad, s, MASKED)
            m = jnp.max(s, axis=-1, keepdims=True)
            p = jnp.exp(s - m)
            l = jnp.sum(p, axis=-1, keepdims=True)
            acc = _dot(p.astype(BF16), v_meta)

            def kv_block(kj, carry, masked):
                m, l, acc = carry
                ks = pl.multiple_of(kj * tq, tq)
                s = _dot_nt(q, kb_ref[pl.ds(ks, tq), :]) + (f_q - fk_ref[0, hh, kj])
                if masked:
                    s = jnp.where(col <= row, s, MASKED)
                m_new = jnp.maximum(m, jnp.max(s, axis=-1, keepdims=True))
                alpha = jnp.exp(m - m_new)
                p = jnp.exp(s - m_new)
                l = alpha * l + jnp.sum(p, axis=-1, keepdims=True)
                acc = alpha * acc + _dot(p.astype(BF16), vb_ref[pl.ds(ks, tq), :])
                return m_new, l, acc

            carry = lax.fori_loop(0, qi, functools.partial(kv_block, masked=False),
                                  (m, l, acc))
            m, l, acc = kv_block(qi, carry, masked=True)
            gate = sz_ref[pl.ds(qs, tq), sl].astype(F32)
            o_ref[pl.ds(qs, tq), sl] = (acc / l * gate).astype(o_ref.dtype)
            return 0

        lax.fori_loop(0, nq, q_block, 0)


def attention_prompt(q, k, v, k_small, v_small, f_q, f_k, f_k_meta, sz, *, n_prompts,
                     n_meta, heads_per_step=2, tq=256):
    rows, d = q.shape
    lq = rows // n_prompts
    n_heads = d // HEAD_DIM
    w = heads_per_step * HEAD_DIM
    big = pl.BlockSpec((lq, w), lambda b, h: (b, h))
    small = pl.BlockSpec((n_meta, w), lambda b, h: (0, h))
    return pl.pallas_call(
        functools.partial(_attn_prompt_kernel, heads_per_step=heads_per_step,
                          n_meta=n_meta, tq=tq),
        grid=(n_prompts, n_heads // heads_per_step),
        in_specs=[big, big, big, small, small,
                  pl.BlockSpec((1, lq, n_heads), lambda b, h: (b, 0, 0)),
                  pl.BlockSpec((1, heads_per_step, lq // tq, 1, tq),
                               lambda b, h: (b, h, 0, 0, 0)),
                  pl.BlockSpec((1, heads_per_step, 1, LANES), lambda b, h: (b, h, 0, 0)),
                  big],
        out_specs=big,
        out_shape=jax.ShapeDtypeStruct((rows, d), BF16),
        scratch_shapes=[pltpu.VMEM((lq, HEAD_DIM), BF16), pltpu.VMEM((lq, HEAD_DIM), BF16)],
        compiler_params=_params("parallel", "parallel"),
        name="attention_prompt",
    )(q, k, v, k_small, v_small, f_q, f_k, f_k_meta, sz)


def _attn_decode_kernel(pt_ref, *refs, pages, n_chunks, n_new, n_heads):
    k_refs = refs[0:pages]
    v_refs = refs[pages:2 * pages]
    lf_refs = refs[2 * pages:3 * pages]
    (qbd_ref, kn_ref, vn_ref, lfn_ref, sz_ref, o_ref,
     kc_ref, vc_ref, acc_ref, m_ref, l_ref, crow_ref, tail_ref) = refs[3 * pages:]
    c = pl.program_id(1)
    page, d = k_refs[0].shape[1], k_refs[0].shape[2]
    ncol = n_heads * n_new
    qbd = qbd_ref[0]

    def to_col(x):
        return jnp.broadcast_to(x, (LANES, ncol)).T[:, 0:1]

    @pl.when(c == 0)
    def _():
        rows_new = lfn_ref.shape[1]
        r = lax.broadcasted_iota(jnp.int32, (rows_new, ncol), 0)
        qcol = lax.broadcasted_iota(jnp.int32, (rows_new, ncol), 1) & (n_new - 1)
        lfn = lfn_ref[0]
        cum = jnp.zeros((rows_new, ncol), F32)
        for j in range(n_new):
            cum = cum + jnp.where(r >= j, lfn[j:j + 1, :], 0.0)
        crow = jnp.sum(jnp.where(r == qcol, cum, 0.0), axis=0, keepdims=True)
        s = _dot(kn_ref[0], qbd) + (crow - cum)
        s = jnp.where(r <= qcol, s, MASKED)
        m = jnp.max(s, axis=0, keepdims=True)
        p = jnp.exp(s - m)
        m_ref[...] = m
        l_ref[...] = jnp.sum(p, axis=0, keepdims=True)
        acc_ref[...] = _dot(p.T.astype(BF16), vn_ref[0])
        crow_ref[...] = crow
        tail_ref[...] = jnp.zeros_like(tail_ref)

    ri = lax.broadcasted_iota(jnp.int32, (page, page), 0)
    ci = lax.broadcasted_iota(jnp.int32, (page, page), 1)
    upper = jnp.where(ci > ri, 1.0, 0.0).astype(BF16)
    er = lax.broadcasted_iota(jnp.int32, (LANES, ncol), 0)
    ec = lax.broadcasted_iota(jnp.int32, (LANES, ncol), 1)
    expand = jnp.where(er * n_new == (ec & ~(n_new - 1)), 1.0, 0.0).astype(BF16)
    tail = tail_ref[...]
    decay = [None] * pages
    for p_i in reversed(range(pages)):
        parts = [_dot(x, expand).astype(BF16) for x in _split3(lf_refs[p_i][0])]
        g = tail
        for e in parts:
            g = g + _dot(upper, e)
        decay[p_i] = g
        for e in parts:
            tail = tail + jnp.sum(e.astype(F32), axis=0, keepdims=True)
        kc_ref[p_i * page:(p_i + 1) * page, :] = k_refs[p_i][0].astype(BF16)
        vc_ref[p_i * page:(p_i + 1) * page, :] = v_refs[p_i][0].astype(BF16)
    tail_ref[...] = tail

    s = _dot(kc_ref[...], qbd) + (crow_ref[...] + jnp.concatenate(decay, axis=0))
    m_old = m_ref[...]
    m_new = jnp.maximum(m_old, jnp.max(s, axis=0, keepdims=True))
    alpha = jnp.exp(m_old - m_new)
    p = jnp.exp(s - m_new)
    m_ref[...] = m_new
    l_ref[...] = alpha * l_ref[...] + jnp.sum(p, axis=0, keepdims=True)
    acc_ref[...] = acc_ref[...] * to_col(alpha) + _dot(p.T.astype(BF16), vc_ref[...])

    @pl.when(c == n_chunks - 1)
    def _():
        inv_l = to_col(1.0 / l_ref[...])
        outs = []
        for h in range(n_heads):
            blk = acc_ref[h * n_new:(h + 1) * n_new, h * HEAD_DIM:(h + 1) * HEAD_DIM]
            outs.append(blk * inv_l[h * n_new:(h + 1) * n_new, :])
        o = jnp.concatenate(outs, axis=1)
        o_ref[0] = (o * sz_ref[0].astype(F32)).astype(o_ref.dtype)


def attention_decode(page_table, cache_k, cache_v, cache_lf, qbd, k_new, v_new, lf_new, sz,
                     *, n_new, pages=4):
    nb, n_pages = page_table.shape
    page, d = cache_k.shape[1], cache_k.shape[2]
    n_heads = d // HEAD_DIM
    ncol = n_heads * n_new
    n_chunks = n_pages // pages

    def paged(width):
        return [pl.BlockSpec(
            (1, page, width),
            functools.partial(lambda p, b, c, pt: (pt[b, (n_chunks - 1 - c) * pages + p], 0, 0), p))
            for p in range(pages)]

    per_seq = lambda shape: pl.BlockSpec((1,) + shape, lambda b, c, pt: (b, 0, 0))
    grid_spec = pltpu.PrefetchScalarGridSpec(
        num_scalar_prefetch=1,
        grid=(nb, n_chunks),
        in_specs=paged(d) + paged(d) + paged(LANES) + [
            per_seq((d, ncol)), per_seq((LANES, d)), per_seq((LANES, d)),
            per_seq((LANES, ncol)), per_seq((n_new, d))],
        out_specs=per_seq((n_new, d)),
        scratch_shapes=[pltpu.VMEM((pages * page, d), BF16), pltpu.VMEM((pages * page, d), BF16),
                        pltpu.VMEM((ncol, d), F32), pltpu.VMEM((1, ncol), F32),
                        pltpu.VMEM((1, ncol), F32), pltpu.VMEM((1, ncol), F32),
                        pltpu.VMEM((1, ncol), F32)],
    )
    return pl.pallas_call(
        functools.partial(_attn_decode_kernel, pages=pages, n_chunks=n_chunks, n_new=n_new,
                          n_heads=n_heads),
        grid_spec=grid_spec,
        out_shape=jax.ShapeDtypeStruct((nb, n_new, d), BF16),
        compiler_params=_params("parallel", "arbitrary"),
        name="attention_decode",
    )(page_table, *([cache_k] * pages), *([cache_v] * pages), *([cache_lf] * pages),
      qbd, k_new, v_new, lf_new, sz)


def kernel(x_prompt, x_sample, state_conv, cache_k, cache_v, cache_logf, page_table,
           meta_tokens, a_pre_g, a_post_g, a_w_in, a_conv_w, a_w_out, kv_g, w_kvf, b_f,
           b_pre_g, b_post_g, b_w_qz, b_w_o):
    n_prompts, seq, d = x_prompt.shape
    n_dec, n_new, _ = x_sample.shape
    n_meta = meta_tokens.shape[0]
    d_attn = b_w_o.shape[1]
    n_heads = d_attn // HEAD_DIM
    dc = a_conv_w.shape[2]
    n_small = n_meta + n_dec * n_new
    scale = HEAD_DIM ** -0.5

    w_in = a_w_in[0].astype(BF16)
    w_out = a_w_out[0].astype(BF16)
    w_k = w_kvf[:, :d_attn].astype(BF16)
    w_v = w_kvf[:, d_attn:2 * d_attn].astype(BF16)
    w_f = jnp.pad(w_kvf[:, 2 * d_attn:], ((0, 0), (0, LANES - n_heads))).astype(BF16)
    bias_f = jnp.pad(b_f, (0, LANES - n_heads)).reshape(1, LANES)
    w_q = b_w_qz[0][:, :d_attn].astype(BF16)
    w_z = b_w_qz[0][:, d_attn:].astype(BF16)
    w_o = b_w_o[0].astype(BF16)

    x_big = x_prompt.reshape(n_prompts * seq, d)
    x_small = jnp.concatenate([meta_tokens, x_sample.reshape(n_dec * n_new, d)], axis=0)

    starts = jnp.arange(n_small)
    in_dec = starts >= n_meta
    pos = jnp.where(in_dec, (starts - n_meta) % n_new, starts)
    m1 = (pos >= 1).astype(F32).reshape(n_small, 1)
    m2 = (pos >= 2).astype(F32).reshape(n_small, 1)
    st = state_conv[0]
    p1 = jnp.zeros((n_dec, n_new, dc), F32).at[:, 0].set(st[:, 1])
    p2 = jnp.zeros((n_dec, n_new, dc), F32).at[:, 0].set(st[:, 0]).at[:, 1].set(st[:, 1])
    zeros_meta = jnp.zeros((n_meta, dc), F32)
    p1 = jnp.concatenate([zeros_meta, p1.reshape(n_dec * n_new, dc)], axis=0)
    p2 = jnp.concatenate([zeros_meta, p2.reshape(n_dec * n_new, dc)], axis=0)

    (xn_small,) = rms_norm_bf16(x_small, a_pre_g)
    g_small, u_small = conv_mixer_small(xn_small, w_in, a_conv_w[0], m1, m2, p1, p2)
    (xn_big,) = rms_norm_bf16(x_big, a_pre_g)
    ctx = jnp.zeros((CARRY_ROWS, dc), F32).at[CARRY_ROWS - 2:].set(u_small[n_meta - 2:n_meta])
    g_big, tails = conv_mixer_big(xn_big, w_in, a_conv_w[0], ctx, seq)
    h_small = proj_residual(g_small, w_out, x_small, a_post_g)
    h_big = proj_residual(g_big, w_out, x_big, a_post_g)

    state_conv_prompt = tails[None, :, CARRY_ROWS - 2:, :]
    u_dec = u_small[n_meta:].reshape(n_dec, n_new, dc)
    state_conv_sample = u_dec[None, :, n_new - 2:, :]

    gains = jnp.stack([kv_g, b_pre_g[0]], axis=0)
    hkv_small, hb_small = rms_norm_bf16(h_small, gains)
    hkv_big, hb_big = rms_norm_bf16(h_big, gains)

    def project(hkv, hb):
        k = matmul(hkv, w_k, F32)
        v = matmul(hkv, w_v, F32)
        lf = forget_gate(hkv, w_f, bias_f)
        q = matmul(hb, w_q, BF16, epilogue=lambda y: y * scale)
        sz = matmul(hb, w_z, BF16, epilogue=_silu)
        return k, v, lf, q, sz

    k_small, v_small, lf_small, q_small, sz_small = project(hkv_small, hb_small)
    k_big, v_big, lf_big, q_big, sz_big = project(hkv_big, hb_big)

    lf_meta = lf_small[:n_meta]
    lf_prompt = jnp.concatenate(
        [jnp.broadcast_to(lf_meta[None], (n_prompts, n_meta, LANES)),
         lf_big.reshape(n_prompts, seq, LANES)], axis=1)
    total = n_meta + seq
    padded = -(-total // LANES) * LANES
    f_prompt = cumsum_rows(jnp.pad(lf_prompt, ((0, 0), (0, padded - total), (0, 0))))

    tq = 256
    f_q = f_prompt[:, n_meta:total, :n_heads]
    f_k = jnp.transpose(f_q, (0, 2, 1)).reshape(n_prompts, n_heads, seq // tq, 1, tq)
    f_k_meta = jnp.pad(jnp.transpose(f_prompt[:, :n_meta, :n_heads], (0, 2, 1)),
                       ((0, 0), (0, 0), (LANES - n_meta, 0))).reshape(n_prompts, n_heads, 1, LANES)
    og_big = attention_prompt(q_big, k_big, v_big, k_small, v_small, f_q, f_k, f_k_meta,
                              sz_big, n_prompts=n_prompts, n_meta=n_meta, tq=tq)
    y_prompt = proj_residual(og_big, w_o, h_big, b_post_g).reshape(n_prompts, seq, d)

    n_rows_dec = n_dec * n_new
    q_dec = q_small[n_meta:].reshape(n_dec, n_new, n_heads, HEAD_DIM)
    eye = jnp.eye(n_heads, dtype=BF16)
    qbd = jnp.einsum('bqhd,hg->bhdgq', q_dec, eye).reshape(n_dec, d_attn, n_heads * n_new)
    pad_rows = ((0, 0), (0, LANES - n_new), (0, 0))
    k_new = jnp.pad(k_small[n_meta:].reshape(n_dec, n_new, d_attn).astype(BF16), pad_rows)
    v_new = jnp.pad(v_small[n_meta:].reshape(n_dec, n_new, d_attn).astype(BF16), pad_rows)
    lf_dec = lf_small[n_meta:, :n_heads].reshape(n_dec, n_new, n_heads)
    lf_new = jnp.pad(jnp.repeat(lf_dec, n_new, axis=2), pad_rows)
    pool, page = cache_k.shape[0], cache_k.shape[1]
    cache_lf = jnp.pad(cache_logf, ((0, 0), (0, 0), (0, LANES - n_heads)))
    og_dec = attention_decode(
        page_table, cache_k.reshape(pool, page, d_attn), cache_v.reshape(pool, page, d_attn),
        cache_lf, qbd, k_new, v_new, lf_new, sz_small[n_meta:].reshape(n_dec, n_new, d_attn),
        n_new=n_new)
    y_sample = proj_residual(og_dec.reshape(n_rows_dec, d_attn), w_o, h_small[n_meta:],
                             b_post_g).reshape(n_dec, n_new, d)

    def with_meta(small, big):
        meta = jnp.broadcast_to(small[None, :n_meta], (n_prompts, n_meta, d_attn))
        full = jnp.concatenate([meta, big.reshape(n_prompts, seq, d_attn)], axis=1)
        return full.reshape(n_prompts, total, n_heads, HEAD_DIM)

    k_prompt = with_meta(k_small, k_big)
    v_prompt = with_meta(v_small, v_big)
    logf_prompt = lf_prompt[:, :, :n_heads]
    k_sample = k_small[n_meta:].reshape(n_dec, n_new, n_heads, HEAD_DIM)
    v_sample = v_small[n_meta:].reshape(n_dec, n_new, n_heads, HEAD_DIM)
    logf_sample = lf_dec
    return (y_prompt, y_sample, state_conv_prompt, state_conv_sample,
            k_prompt, v_prompt, logf_prompt, k_sample, v_sample, logf_sample)
```

```python
import functools

import jax
import jax.numpy as jnp
from jax import lax
from jax.experimental import pallas as pl
from jax.experimental.pallas import tpu as pltpu

F32 = jnp.float32
BF16 = jnp.bfloat16

RMS_EPS = 1e-6
CONV_WIDTH = 3
HEAD_DIM = 128
LANES = 128
CARRY_ROWS = 8
MASKED = -1e30
VMEM_LIMIT_BYTES = 60 * 1024 * 1024


def _params(*sem):
    return pltpu.CompilerParams(dimension_semantics=sem, vmem_limit_bytes=VMEM_LIMIT_BYTES)


def _dot(a, b):
    return jnp.dot(a, b, preferred_element_type=F32)


def _dot_nt(a, b):
    return lax.dot_general(a, b, (((1,), (1,)), ((), ())), preferred_element_type=F32)


def _split3(x):
    hi = x.astype(BF16)
    r1 = x - hi.astype(F32)
    mid = r1.astype(BF16)
    lo = (r1 - mid.astype(F32)).astype(BF16)
    return hi, mid, lo


def _sigmoid(z):
    return 1.0 / (1.0 + jnp.exp(-z))


def _row_tile(rows, want):
    return want if rows % want == 0 else rows


def _norm_kernel(x_ref, g_ref, *o_refs):
    x = x_ref[...]
    xhat = x * lax.rsqrt(jnp.mean(x * x, axis=-1, keepdims=True) + RMS_EPS)
    for n, o_ref in enumerate(o_refs):
        o_ref[...] = (xhat * g_ref[n:n + 1, :]).astype(o_ref.dtype)


def rms_norm_bf16(x, gains):
    rows, d = x.shape
    n = gains.shape[0]
    tm = _row_tile(rows, 512)
    return pl.pallas_call(
        _norm_kernel,
        grid=(rows // tm,),
        in_specs=[pl.BlockSpec((tm, d), lambda i: (i, 0)),
                  pl.BlockSpec((n, d), lambda i: (0, 0))],
        out_specs=[pl.BlockSpec((tm, d), lambda i: (i, 0))] * n,
        out_shape=[jax.ShapeDtypeStruct((rows, d), BF16)] * n,
        compiler_params=_params("parallel"),
        name="rms_norm",
    )(x, gains)


def _gated_conv(x, wb_ref, wc_ref, wh_ref, wz_ref, cw_ref, um1_fix, um2_fix):
    u = _dot(x, wc_ref[...]) * _dot(x, wh_ref[...])
    um1 = um1_fix(pltpu.roll(u, 1, 0))
    um2 = um2_fix(pltpu.roll(u, 2, 0))
    cw = cw_ref[...]
    y = cw[0:1, :] * um2 + cw[1:2, :] * um1 + cw[2:3, :] * u
    z = _dot(x, wz_ref[...])
    g = _dot(x, wb_ref[...]) * y * (z * _sigmoid(z))
    return u, g


def _conv_big_kernel(x_ref, wb_ref, wc_ref, wh_ref, wz_ref, cw_ref, ctx_ref,
                     g_ref, tail_ref, carry_ref, *, tiles_per_seq):
    i = pl.program_id(1)

    @pl.when(i % tiles_per_seq == 0)
    def _():
        carry_ref[...] = ctx_ref[...]

    carry = carry_ref[...]
    prev1 = carry[CARRY_ROWS - 1:CARRY_ROWS, :]
    prev2 = carry[CARRY_ROWS - 2:CARRY_ROWS - 1, :]
    tm, tn = g_ref.shape
    row = lax.broadcasted_iota(jnp.int32, (tm, tn), 0)
    u, g = _gated_conv(
        x_ref[...], wb_ref, wc_ref, wh_ref, wz_ref, cw_ref,
        lambda r1: jnp.where(row == 0, prev1, r1),
        lambda r2: jnp.where(row == 0, prev2, jnp.where(row == 1, prev1, r2)))
    tail = u[tm - CARRY_ROWS:, :]
    carry_ref[...] = tail
    tail_ref[0] = tail
    g_ref[...] = g.astype(g_ref.dtype)


def _conv_small_kernel(x_ref, wb_ref, wc_ref, wh_ref, wz_ref, cw_ref,
                       m1_ref, m2_ref, p1_ref, p2_ref, g_ref, u_ref):
    u, g = _gated_conv(
        x_ref[...], wb_ref, wc_ref, wh_ref, wz_ref, cw_ref,
        lambda r1: r1 * m1_ref[...] + p1_ref[...],
        lambda r2: r2 * m2_ref[...] + p2_ref[...])
    u_ref[...] = u
    g_ref[...] = g.astype(g_ref.dtype)


def _w_in_specs(d, dc, tn):
    nj = dc // tn
    return [pl.BlockSpec((d, tn), functools.partial(lambda c, j, i: (0, c * nj + j), c))
            for c in range(4)]


def conv_mixer_big(xn, w_in, conv_w, ctx, seq_len, tm=1024, tn=256):
    rows, d = xn.shape
    dc = w_in.shape[1] // 4
    tm = min(tm, seq_len)
    tps = seq_len // tm
    n_seq = rows // seq_len
    return pl.pallas_call(
        functools.partial(_conv_big_kernel, tiles_per_seq=tps),
        grid=(dc // tn, rows // tm),
        in_specs=[pl.BlockSpec((tm, d), lambda j, i: (i, 0))] + _w_in_specs(d, dc, tn) + [
            pl.BlockSpec((CONV_WIDTH, tn), lambda j, i: (0, j)),
            pl.BlockSpec((CARRY_ROWS, tn), lambda j, i: (0, j))],
        out_specs=[pl.BlockSpec((tm, tn), lambda j, i: (i, j)),
                   pl.BlockSpec((1, CARRY_ROWS, tn), lambda j, i: (i // tps, 0, j))],
        out_shape=[jax.ShapeDtypeStruct((rows, dc), BF16),
                   jax.ShapeDtypeStruct((n_seq, CARRY_ROWS, dc), F32)],
        scratch_shapes=[pltpu.VMEM((CARRY_ROWS, tn), F32)],
        compiler_params=_params("arbitrary", "arbitrary"),
        name="conv_mixer_big",
    )(xn, w_in, w_in, w_in, w_in, conv_w, ctx)


def conv_mixer_small(xn, w_in, conv_w, m1, m2, p1, p2, tn=256):
    rows, d = xn.shape
    dc = w_in.shape[1] // 4
    return pl.pallas_call(
        _conv_small_kernel,
        grid=(dc // tn, 1),
        in_specs=[pl.BlockSpec((rows, d), lambda j, i: (0, 0))] + _w_in_specs(d, dc, tn) + [
            pl.BlockSpec((CONV_WIDTH, tn), lambda j, i: (0, j)),
            pl.BlockSpec((rows, 1), lambda j, i: (0, 0)),
            pl.BlockSpec((rows, 1), lambda j, i: (0, 0)),
            pl.BlockSpec((rows, tn), lambda j, i: (0, j)),
            pl.BlockSpec((rows, tn), lambda j, i: (0, j))],
        out_specs=[pl.BlockSpec((rows, tn), lambda j, i: (0, j)),
                   pl.BlockSpec((rows, tn), lambda j, i: (0, j))],
        out_shape=[jax.ShapeDtypeStruct((rows, dc), BF16),
                   jax.ShapeDtypeStruct((rows, dc), F32)],
        compiler_params=_params("parallel", "arbitrary"),
        name="conv_mixer_small",
    )(xn, w_in, w_in, w_in, w_in, conv_w, m1, m2, p1, p2)


def _proj_residual_kernel(x_ref, w_ref, h_ref, g_ref, o_ref):
    a = _dot(x_ref[...], w_ref[...])
    inv = lax.rsqrt(jnp.mean(a * a, axis=-1, keepdims=True) + RMS_EPS)
    o_ref[...] = h_ref[...] + a * inv * g_ref[...]


def proj_residual(x, w, h, gain, tm=256):
    rows, kdim = x.shape
    d = w.shape[1]
    tm = _row_tile(rows, tm)
    return pl.pallas_call(
        _proj_residual_kernel,
        grid=(rows // tm,),
        in_specs=[pl.BlockSpec((tm, kdim), lambda i: (i, 0)),
                  pl.BlockSpec((kdim, d), lambda i: (0, 0), pipeline_mode=pl.Buffered(1)),
                  pl.BlockSpec((tm, d), lambda i: (i, 0)),
                  pl.BlockSpec((1, d), lambda i: (0, 0))],
        out_specs=pl.BlockSpec((tm, d), lambda i: (i, 0)),
        out_shape=jax.ShapeDtypeStruct((rows, d), F32),
        compiler_params=_params("arbitrary"),
        name="proj_residual",
    )(x, w, h, gain)


def _mm_kernel(x_ref, w_ref, o_ref, *, epilogue):
    o_ref[...] = epilogue(_dot(x_ref[...], w_ref[...])).astype(o_ref.dtype)


def matmul(x, w, out_dtype, epilogue=lambda y: y, tm=1024, tn=1024):
    rows, kdim = x.shape
    n = w.shape[1]
    tm = _row_tile(rows, tm)
    tn = min(tn, n)
    return pl.pallas_call(
        functools.partial(_mm_kernel, epilogue=epilogue),
        grid=(n // tn, rows // tm),
        in_specs=[pl.BlockSpec((tm, kdim), lambda j, i: (i, 0)),
                  pl.BlockSpec((kdim, tn), lambda j, i: (0, j))],
        out_specs=pl.BlockSpec((tm, tn), lambda j, i: (i, j)),
        out_shape=jax.ShapeDtypeStruct((rows, n), out_dtype),
        compiler_params=_params("parallel", "parallel"),
        name="matmul",
    )(x, w)


def _silu(z):
    return z * _sigmoid(z)


def _forget_gate_kernel(x_ref, w_ref, b_ref, o_ref):
    t = _dot(x_ref[...], w_ref[...]) + b_ref[...]
    o_ref[...] = jnp.minimum(t, 0.0) - jnp.log(1.0 + jnp.exp(-jnp.abs(t)))


def forget_gate(x, w_f, b_f):
    rows, kdim = x.shape
    tm = _row_tile(rows, 1024)
    return pl.pallas_call(
        _forget_gate_kernel,
        grid=(rows // tm,),
        in_specs=[pl.BlockSpec((tm, kdim), lambda i: (i, 0)),
                  pl.BlockSpec((kdim, LANES), lambda i: (0, 0)),
                  pl.BlockSpec((1, LANES), lambda i: (0, 0))],
        out_specs=pl.BlockSpec((tm, LANES), lambda i: (i, 0)),
        out_shape=jax.ShapeDtypeStruct((rows, LANES), F32),
        compiler_params=_params("parallel"),
        name="forget_gate",
    )(x, w_f, b_f)


def _cumsum_kernel(x_ref, o_ref, *, n_chunks):
    r = lax.broadcasted_iota(jnp.int32, (LANES, LANES), 0)
    c = lax.broadcasted_iota(jnp.int32, (LANES, LANES), 1)
    tri = jnp.where(c <= r, 1.0, 0.0).astype(BF16)
    carry = jnp.zeros((1, LANES), F32)
    for n in range(n_chunks):
        hi, mid, lo = _split3(x_ref[0, n * LANES:(n + 1) * LANES, :])
        f = _dot(tri, hi) + _dot(tri, mid) + _dot(tri, lo) + carry
        o_ref[0, n * LANES:(n + 1) * LANES, :] = f
        carry = f[LANES - 1:LANES, :]


def cumsum_rows(x):
    b, length, lanes = x.shape
    return pl.pallas_call(
        functools.partial(_cumsum_kernel, n_chunks=length // LANES),
        grid=(b,),
        in_specs=[pl.BlockSpec((1, length, lanes), lambda i: (i, 0, 0))],
        out_specs=pl.BlockSpec((1, length, lanes), lambda i: (i, 0, 0)),
        out_shape=jax.ShapeDtypeStruct(x.shape, F32),
        compiler_params=_params("parallel"),
        name="cumsum_rows",
    )(x)


def _attn_prompt_kernel(q_ref, k_ref, v_ref, km_ref, vm_ref, f_ref, fm_ref, frow_ref,
                        sz_ref, o_ref, kb_ref, vt_ref, fcol_ref, *, heads_per_step, n_meta, tq):
    hg = pl.program_id(1)
    lq = q_ref.shape[0]
    nq = lq // tq
    n_heads = f_ref.shape[2]
    pad = LANES - n_meta
    key = lax.broadcasted_iota(jnp.int32, (tq, tq), 0)
    qry = lax.broadcasted_iota(jnp.int32, (tq, tq), 1)
    pad_row = lax.broadcasted_iota(jnp.int32, (LANES, 1), 0) < pad

    for hh in range(heads_per_step):
        sl = slice(hh * HEAD_DIM, (hh + 1) * HEAD_DIM)
        head = hg * heads_per_step + hh
        zeros = jnp.zeros((pad, HEAD_DIM), F32)
        kb_ref[hh, 0:LANES, :] = jnp.concatenate([zeros, km_ref[:, sl]], axis=0).astype(BF16)
        kb_ref[hh, LANES:, :] = k_ref[:, sl].astype(BF16)
        vt_ref[hh, :, 0:LANES] = jnp.concatenate([zeros, vm_ref[:, sl]], axis=0).T.astype(BF16)
        for c in range(nq):
            vt_ref[hh, :, LANES + c * tq:LANES + (c + 1) * tq] = (
                v_ref[c * tq:(c + 1) * tq, sl].T.astype(BF16))
        pick_m = lax.broadcasted_iota(jnp.int32, (LANES, n_heads), 1) == head
        f_meta = jnp.sum(jnp.where(pick_m, fm_ref[0], 0.0), axis=-1, keepdims=True)
        fcol_ref[hh, 0:LANES, :] = jnp.where(pad_row, -MASKED, f_meta)
        pick = lax.broadcasted_iota(jnp.int32, (lq, n_heads), 1) == head
        fcol_ref[hh, LANES:, :] = jnp.sum(jnp.where(pick, f_ref[0], 0.0), axis=-1, keepdims=True)

    for qi in range(nq):
        nk = LANES + (qi + 1) * tq
        rows = slice(qi * tq, (qi + 1) * tq)
        for hh in range(heads_per_step):
            sl = slice(hh * HEAD_DIM, (hh + 1) * HEAD_DIM)
            s = _dot_nt(kb_ref[hh, 0:nk, :], q_ref[rows, sl])
            s = s + (frow_ref[0, hh, qi] - fcol_ref[hh, 0:nk, :])
            s = jnp.concatenate(
                [s[:nk - tq], jnp.where(key <= qry, s[nk - tq:], MASKED)], axis=0)
            m = jnp.max(s, axis=0, keepdims=True)
            p = jnp.exp(s - m)
            l = jnp.sum(p, axis=0, keepdims=True)
            acc = _dot(vt_ref[hh, :, 0:nk], p.astype(BF16))
            gate = sz_ref[rows, sl].astype(F32)
            o_ref[rows, sl] = ((acc / l).T * gate).astype(o_ref.dtype)


def attention_prompt(q, k, v, k_small, v_small, f_nat, f_meta, f_rows, sz, *, n_prompts,
                     n_meta, heads_per_step=2, tq=256):
    rows, d = q.shape
    lq = rows // n_prompts
    n_heads = d // HEAD_DIM
    w = heads_per_step * HEAD_DIM
    big = pl.BlockSpec((lq, w), lambda b, h: (b, h))
    small = pl.BlockSpec((n_meta, w), lambda b, h: (0, h))
    return pl.pallas_call(
        functools.partial(_attn_prompt_kernel, heads_per_step=heads_per_step,
                          n_meta=n_meta, tq=tq),
        grid=(n_prompts, n_heads // heads_per_step),
        in_specs=[big, big, big, small, small,
                  pl.BlockSpec((1, lq, n_heads), lambda b, h: (b, 0, 0)),
                  pl.BlockSpec((1, LANES, n_heads), lambda b, h: (b, 0, 0)),
                  pl.BlockSpec((1, heads_per_step, lq // tq, 1, tq),
                               lambda b, h: (b, h, 0, 0, 0)),
                  big],
        out_specs=big,
        out_shape=jax.ShapeDtypeStruct((rows, d), BF16),
        scratch_shapes=[pltpu.VMEM((heads_per_step, LANES + lq, HEAD_DIM), BF16),
                        pltpu.VMEM((heads_per_step, HEAD_DIM, LANES + lq), BF16),
                        pltpu.VMEM((heads_per_step, LANES + lq, 1), F32)],
        compiler_params=_params("parallel", "parallel"),
        name="attention_prompt",
    )(q, k, v, k_small, v_small, f_nat, f_meta, f_rows, sz)


def _attn_decode_kernel(pt_ref, *refs, pages, n_chunks, n_new, n_heads):
    k_refs = refs[0:pages]
    v_refs = refs[pages:2 * pages]
    lf_refs = refs[2 * pages:3 * pages]
    (qbd_ref, kn_ref, vn_ref, lfn_ref, sz_ref, o_ref,
     kc_ref, vc_ref, acc_ref, m_ref, l_ref, crow_ref, tail_ref) = refs[3 * pages:]
    c = pl.program_id(1)
    page = lf_refs[0].shape[1]
    ncol = n_heads * n_new
    qbd = qbd_ref[0]

    def to_col(x):
        return jnp.broadcast_to(x, (LANES, ncol)).T[:, 0:1]

    @pl.when(c == 0)
    def _():
        rows_new = lfn_ref.shape[1]
        r = lax.broadcasted_iota(jnp.int32, (rows_new, ncol), 0)
        qcol = lax.broadcasted_iota(jnp.int32, (rows_new, ncol), 1) & (n_new - 1)
        lfn = lfn_ref[0]
        cum = jnp.zeros((rows_new, ncol), F32)
        for j in range(n_new):
            cum = cum + jnp.where(r >= j, lfn[j:j + 1, :], 0.0)
        crow = jnp.sum(jnp.where(r == qcol, cum, 0.0), axis=0, keepdims=True)
        s = _dot(kn_ref[0], qbd) + (crow - cum)
        s = jnp.where(r <= qcol, s, MASKED)
        m = jnp.max(s, axis=0, keepdims=True)
        p = jnp.exp(s - m)
        m_ref[...] = m
        l_ref[...] = jnp.sum(p, axis=0, keepdims=True)
        acc_ref[...] = _dot(p.T.astype(BF16), vn_ref[0])
        crow_ref[...] = crow
        tail_ref[...] = jnp.zeros_like(tail_ref)

    ri = lax.broadcasted_iota(jnp.int32, (page, page), 0)
    ci = lax.broadcasted_iota(jnp.int32, (page, page), 1)
    upper = jnp.where(ci > ri, 1.0, 0.0).astype(BF16)
    er = lax.broadcasted_iota(jnp.int32, (LANES, ncol), 0)
    ec = lax.broadcasted_iota(jnp.int32, (LANES, ncol), 1)
    expand = jnp.where(er * n_new == (ec & ~(n_new - 1)), 1.0, 0.0).astype(BF16)
    tail = tail_ref[...]
    decay = [None] * pages
    for p_i in reversed(range(pages)):
        parts = [_dot(x, expand).astype(BF16) for x in _split3(lf_refs[p_i][0])]
        g = tail
        for e in parts:
            g = g + _dot(upper, e)
        decay[p_i] = g
        for e in parts:
            tail = tail + jnp.sum(e.astype(F32), axis=0, keepdims=True)
        rows = slice(p_i * page, (p_i + 1) * page)
        k_heads = pltpu.einshape("khd->hkd", k_refs[p_i][0]).astype(BF16)
        v_heads = pltpu.einshape("khd->hkd", v_refs[p_i][0]).astype(BF16)
        for h in range(n_heads):
            cols = slice(h * HEAD_DIM, (h + 1) * HEAD_DIM)
            kc_ref[rows, cols] = k_heads[h]
            vc_ref[rows, cols] = v_heads[h]
    tail_ref[...] = tail

    s = _dot(kc_ref[...], qbd) + (crow_ref[...] + jnp.concatenate(decay, axis=0))
    m_old = m_ref[...]
    m_new = jnp.maximum(m_old, jnp.max(s, axis=0, keepdims=True))
    alpha = jnp.exp(m_old - m_new)
    p = jnp.exp(s - m_new)
    m_ref[...] = m_new
    l_ref[...] = alpha * l_ref[...] + jnp.sum(p, axis=0, keepdims=True)
    acc_ref[...] = acc_ref[...] * to_col(alpha) + _dot(p.T.astype(BF16), vc_ref[...])

    @pl.when(c == n_chunks - 1)
    def _():
        inv_l = to_col(1.0 / l_ref[...])
        outs = []
        for h in range(n_heads):
            blk = acc_ref[h * n_new:(h + 1) * n_new, h * HEAD_DIM:(h + 1) * HEAD_DIM]
            outs.append(blk * inv_l[h * n_new:(h + 1) * n_new, :])
        o = jnp.concatenate(outs, axis=1)
        o_ref[0] = (o * sz_ref[0].astype(F32)).astype(o_ref.dtype)


def attention_decode(page_table, cache_k, cache_v, cache_lf, qbd, k_new, v_new, lf_new, sz,
                     *, n_new, pages=4):
    nb, n_pages = page_table.shape
    _, page, n_heads, _ = cache_k.shape
    d = n_heads * HEAD_DIM
    ncol = n_heads * n_new
    n_chunks = n_pages // pages

    def paged(*tail):
        zeros = (0,) * len(tail)
        return [pl.BlockSpec(
            (1,) + tail,
            functools.partial(
                lambda p, b, c, pt: (pt[b, (n_chunks - 1 - c) * pages + p],) + zeros, p))
            for p in range(pages)]

    per_seq = lambda shape: pl.BlockSpec((1,) + shape, lambda b, c, pt: (b, 0, 0))
    kv_pages = paged(page, n_heads, HEAD_DIM)
    grid_spec = pltpu.PrefetchScalarGridSpec(
        num_scalar_prefetch=1,
        grid=(nb, n_chunks),
        in_specs=kv_pages + kv_pages + paged(page, LANES) + [
            per_seq((d, ncol)), per_seq((LANES, d)), per_seq((LANES, d)),
            per_seq((LANES, ncol)), per_seq((n_new, d))],
        out_specs=per_seq((n_new, d)),
        scratch_shapes=[pltpu.VMEM((pages * page, d), BF16), pltpu.VMEM((pages * page, d), BF16),
                        pltpu.VMEM((ncol, d), F32), pltpu.VMEM((1, ncol), F32),
                        pltpu.VMEM((1, ncol), F32), pltpu.VMEM((1, ncol), F32),
                        pltpu.VMEM((1, ncol), F32)],
    )
    return pl.pallas_call(
        functools.partial(_attn_decode_kernel, pages=pages, n_chunks=n_chunks, n_new=n_new,
                          n_heads=n_heads),
        grid_spec=grid_spec,
        out_shape=jax.ShapeDtypeStruct((nb, n_new, d), BF16),
        compiler_params=_params("parallel", "arbitrary"),
        name="attention_decode",
    )(page_table, *([cache_k] * pages), *([cache_v] * pages), *([cache_lf] * pages),
      qbd, k_new, v_new, lf_new, sz)


def kernel(x_prompt, x_sample, state_conv, cache_k, cache_v, cache_logf, page_table,
           meta_tokens, a_pre_g, a_post_g, a_w_in, a_conv_w, a_w_out, kv_g, w_kvf, b_f,
           b_pre_g, b_post_g, b_w_qz, b_w_o):
    n_prompts, seq, d = x_prompt.shape
    n_dec, n_new, _ = x_sample.shape
    n_meta = meta_tokens.shape[0]
    d_attn = b_w_o.shape[1]
    n_heads = d_attn // HEAD_DIM
    dc = a_conv_w.shape[2]
    n_small = n_meta + n_dec * n_new
    scale = HEAD_DIM ** -0.5

    w_in = a_w_in[0].astype(BF16)
    w_out = a_w_out[0].astype(BF16)
    w_k = w_kvf[:, :d_attn].astype(BF16)
    w_v = w_kvf[:, d_attn:2 * d_attn].astype(BF16)
    w_f = jnp.pad(w_kvf[:, 2 * d_attn:], ((0, 0), (0, LANES - n_heads))).astype(BF16)
    bias_f = jnp.pad(b_f, (0, LANES - n_heads)).reshape(1, LANES)
    w_q = b_w_qz[0][:, :d_attn].astype(BF16)
    w_z = b_w_qz[0][:, d_attn:].astype(BF16)
    w_o = b_w_o[0].astype(BF16)

    x_big = x_prompt.reshape(n_prompts * seq, d)
    x_small = jnp.concatenate([meta_tokens, x_sample.reshape(n_dec * n_new, d)], axis=0)

    starts = jnp.arange(n_small)
    in_dec = starts >= n_meta
    pos = jnp.where(in_dec, (starts - n_meta) % n_new, starts)
    m1 = (pos >= 1).astype(F32).reshape(n_small, 1)
    m2 = (pos >= 2).astype(F32).reshape(n_small, 1)
    st = state_conv[0]
    p1 = jnp.zeros((n_dec, n_new, dc), F32).at[:, 0].set(st[:, 1])
    p2 = jnp.zeros((n_dec, n_new, dc), F32).at[:, 0].set(st[:, 0]).at[:, 1].set(st[:, 1])
    zeros_meta = jnp.zeros((n_meta, dc), F32)
    p1 = jnp.concatenate([zeros_meta, p1.reshape(n_dec * n_new, dc)], axis=0)
    p2 = jnp.concatenate([zeros_meta, p2.reshape(n_dec * n_new, dc)], axis=0)

    (xn_small,) = rms_norm_bf16(x_small, a_pre_g)
    g_small, u_small = conv_mixer_small(xn_small, w_in, a_conv_w[0], m1, m2, p1, p2)
    (xn_big,) = rms_norm_bf16(x_big, a_pre_g)
    ctx = jnp.zeros((CARRY_ROWS, dc), F32).at[CARRY_ROWS - 2:].set(u_small[n_meta - 2:n_meta])
    g_big, tails = conv_mixer_big(xn_big, w_in, a_conv_w[0], ctx, seq)
    h_small = proj_residual(g_small, w_out, x_small, a_post_g)
    h_big = proj_residual(g_big, w_out, x_big, a_post_g)

    state_conv_prompt = tails[None, :, CARRY_ROWS - 2:, :]
    u_dec = u_small[n_meta:].reshape(n_dec, n_new, dc)
    state_conv_sample = u_dec[None, :, n_new - 2:, :]

    gains = jnp.stack([kv_g, b_pre_g[0]], axis=0)
    hkv_small, hb_small = rms_norm_bf16(h_small, gains)
    hkv_big, hb_big = rms_norm_bf16(h_big, gains)

    def project(hkv, hb):
        k = matmul(hkv, w_k, F32)
        v = matmul(hkv, w_v, F32)
        lf = forget_gate(hkv, w_f, bias_f)
        q = matmul(hb, w_q, BF16, epilogue=lambda y: y * scale)
        sz = matmul(hb, w_z, BF16, epilogue=_silu)
        return k, v, lf, q, sz

    k_small, v_small, lf_small, q_small, sz_small = project(hkv_small, hb_small)
    k_big, v_big, lf_big, q_big, sz_big = project(hkv_big, hb_big)

    lf_meta = lf_small[:n_meta]
    lf_prompt = jnp.concatenate(
        [jnp.broadcast_to(lf_meta[None], (n_prompts, n_meta, LANES)),
         lf_big.reshape(n_prompts, seq, LANES)], axis=1)
    total = n_meta + seq
    padded = -(-total // LANES) * LANES
    f_prompt = cumsum_rows(jnp.pad(lf_prompt, ((0, 0), (0, padded - total), (0, 0))))

    tq = 256
    f_nat = f_prompt[:, n_meta:total, :n_heads]
    f_rows = jnp.transpose(f_nat, (0, 2, 1)).reshape(n_prompts, n_heads, seq // tq, 1, tq)
    f_meta = jnp.pad(f_prompt[:, :n_meta, :n_heads], ((0, 0), (LANES - n_meta, 0), (0, 0)))
    og_big = attention_prompt(q_big, k_big, v_big, k_small, v_small, f_nat, f_meta, f_rows,
                              sz_big, n_prompts=n_prompts, n_meta=n_meta, tq=tq)
    y_prompt = proj_residual(og_big, w_o, h_big, b_post_g).reshape(n_prompts, seq, d)

    n_rows_dec = n_dec * n_new
    q_dec = q_small[n_meta:].reshape(n_dec, n_new, n_heads, HEAD_DIM)
    eye = jnp.eye(n_heads, dtype=BF16)
    qbd = jnp.einsum('bqhd,hg->bhdgq', q_dec, eye).reshape(n_dec, d_attn, n_heads * n_new)
    pad_rows = ((0, 0), (0, LANES - n_new), (0, 0))
    k_new = jnp.pad(k_small[n_meta:].reshape(n_dec, n_new, d_attn).astype(BF16), pad_rows)
    v_new = jnp.pad(v_small[n_meta:].reshape(n_dec, n_new, d_attn).astype(BF16), pad_rows)
    lf_dec = lf_small[n_meta:, :n_heads].reshape(n_dec, n_new, n_heads)
    lf_new = jnp.pad(jnp.repeat(lf_dec, n_new, axis=2), pad_rows)
    pool, page = cache_k.shape[0], cache_k.shape[1]
    cache_lf = jnp.pad(cache_logf, ((0, 0), (0, 0), (0, LANES - n_heads)))
    og_dec = attention_decode(
        page_table, cache_k, cache_v, cache_lf, qbd, k_new, v_new, lf_new, sz_small[n_meta:].reshape(n_dec, n_new, d_attn),
        n_new=n_new)
    y_sample = proj_residual(og_dec.reshape(n_rows_dec, d_attn), w_o, h_small[n_meta:],
                             b_post_g).reshape(n_dec, n_new, d)

    def with_meta(small, big):
        meta = jnp.broadcast_to(small[None, :n_meta], (n_prompts, n_meta, d_attn))
        full = jnp.concatenate([meta, big.reshape(n_prompts, seq, d_attn)], axis=1)
        return full.reshape(n_prompts, total, n_heads, HEAD_DIM)

    k_prompt = with_meta(k_small, k_big)
    v_prompt = with_meta(v_small, v_big)
    logf_prompt = lf_prompt[:, :, :n_heads]
    k_sample = k_small[n_meta:].reshape(n_dec, n_new, n_heads, HEAD_DIM)
    v_sample = v_small[n_meta:].reshape(n_dec, n_new, n_heads, HEAD_DIM)
    logf_sample = lf_dec
    return (y_prompt, y_sample, state_conv_prompt, state_conv_sample,
            k_prompt, v_prompt, logf_prompt, k_sample, v_sample, logf_sample)
```

```python
import functools

import jax
import jax.numpy as jnp
from jax import lax
from jax.experimental import pallas as pl
from jax.experimental.pallas import tpu as pltpu

F32 = jnp.float32
BF16 = jnp.bfloat16

RMS_EPS = 1e-6
CONV_WIDTH = 3
HEAD_DIM = 128
LANES = 128
CARRY_ROWS = 8
MASKED = -1e30
VMEM_LIMIT_BYTES = 60 * 1024 * 1024


def _params(*sem):
    return pltpu.CompilerParams(dimension_semantics=sem, vmem_limit_bytes=VMEM_LIMIT_BYTES)


def _dot(a, b):
    return jnp.dot(a, b, preferred_element_type=F32)


def _dot_nt(a, b):
    return lax.dot_general(a, b, (((1,), (1,)), ((), ())), preferred_element_type=F32)


def _split3(x):
    hi = x.astype(BF16)
    r1 = x - hi.astype(F32)
    mid = r1.astype(BF16)
    lo = (r1 - mid.astype(F32)).astype(BF16)
    return hi, mid, lo


def _sigmoid(z):
    return 1.0 / (1.0 + jnp.exp(-z))


BF16_ROWS = 16


def _row_tile(rows, want):
    for t in range(min(want, rows) // BF16_ROWS * BF16_ROWS, 0, -BF16_ROWS):
        if rows % t == 0:
            return t
    return rows


def _norm_kernel(x_ref, g_ref, *o_refs):
    x = x_ref[...]
    xhat = x * lax.rsqrt(jnp.mean(x * x, axis=-1, keepdims=True) + RMS_EPS)
    for n, o_ref in enumerate(o_refs):
        o_ref[...] = (xhat * g_ref[n:n + 1, :]).astype(o_ref.dtype)


def rms_norm_bf16(x, gains):
    rows, d = x.shape
    n = gains.shape[0]
    tm = _row_tile(rows, 512)
    return pl.pallas_call(
        _norm_kernel,
        grid=(rows // tm,),
        in_specs=[pl.BlockSpec((tm, d), lambda i: (i, 0)),
                  pl.BlockSpec((n, d), lambda i: (0, 0))],
        out_specs=[pl.BlockSpec((tm, d), lambda i: (i, 0))] * n,
        out_shape=[jax.ShapeDtypeStruct((rows, d), BF16)] * n,
        compiler_params=_params("parallel"),
        name="rms_norm",
    )(x, gains)


def _gated_conv(x, wb_ref, wc_ref, wh_ref, wz_ref, cw_ref, um1_fix, um2_fix):
    u = _dot(x, wc_ref[...]) * _dot(x, wh_ref[...])
    um1 = um1_fix(pltpu.roll(u, 1, 0))
    um2 = um2_fix(pltpu.roll(u, 2, 0))
    cw = cw_ref[...]
    y = cw[0:1, :] * um2 + cw[1:2, :] * um1 + cw[2:3, :] * u
    z = _dot(x, wz_ref[...])
    g = _dot(x, wb_ref[...]) * y * (z * _sigmoid(z))
    return u, g


def _conv_big_kernel(x_ref, wb_ref, wc_ref, wh_ref, wz_ref, cw_ref, ctx_ref,
                     g_ref, tail_ref, carry_ref, *, tiles_per_seq):
    i = pl.program_id(1)

    @pl.when(i % tiles_per_seq == 0)
    def _():
        carry_ref[...] = ctx_ref[...]

    carry = carry_ref[...]
    prev1 = carry[CARRY_ROWS - 1:CARRY_ROWS, :]
    prev2 = carry[CARRY_ROWS - 2:CARRY_ROWS - 1, :]
    tm, tn = g_ref.shape
    row = lax.broadcasted_iota(jnp.int32, (tm, tn), 0)
    u, g = _gated_conv(
        x_ref[...], wb_ref, wc_ref, wh_ref, wz_ref, cw_ref,
        lambda r1: jnp.where(row == 0, prev1, r1),
        lambda r2: jnp.where(row == 0, prev2, jnp.where(row == 1, prev1, r2)))
    tail = u[tm - CARRY_ROWS:, :]
    carry_ref[...] = tail
    tail_ref[0] = tail
    g_ref[...] = g.astype(g_ref.dtype)


def _conv_small_kernel(x_ref, wb_ref, wc_ref, wh_ref, wz_ref, cw_ref,
                       m1_ref, m2_ref, p1_ref, p2_ref, g_ref, u_ref):
    u, g = _gated_conv(
        x_ref[...], wb_ref, wc_ref, wh_ref, wz_ref, cw_ref,
        lambda r1: r1 * m1_ref[...] + p1_ref[...],
        lambda r2: r2 * m2_ref[...] + p2_ref[...])
    u_ref[...] = u
    g_ref[...] = g.astype(g_ref.dtype)


def _w_in_specs(d, dc, tn):
    nj = dc // tn
    return [pl.BlockSpec((d, tn), functools.partial(lambda c, j, i: (0, c * nj + j), c))
            for c in range(4)]


def conv_mixer_big(xn, w_in, conv_w, ctx, seq_len, tm=1024, tn=256):
    rows, d = xn.shape
    dc = w_in.shape[1] // 4
    tm = min(tm, seq_len)
    tps = seq_len // tm
    n_seq = rows // seq_len
    return pl.pallas_call(
        functools.partial(_conv_big_kernel, tiles_per_seq=tps),
        grid=(dc // tn, rows // tm),
        in_specs=[pl.BlockSpec((tm, d), lambda j, i: (i, 0))] + _w_in_specs(d, dc, tn) + [
            pl.BlockSpec((CONV_WIDTH, tn), lambda j, i: (0, j)),
            pl.BlockSpec((CARRY_ROWS, tn), lambda j, i: (0, j))],
        out_specs=[pl.BlockSpec((tm, tn), lambda j, i: (i, j)),
                   pl.BlockSpec((1, CARRY_ROWS, tn), lambda j, i: (i // tps, 0, j))],
        out_shape=[jax.ShapeDtypeStruct((rows, dc), BF16),
                   jax.ShapeDtypeStruct((n_seq, CARRY_ROWS, dc), F32)],
        scratch_shapes=[pltpu.VMEM((CARRY_ROWS, tn), F32)],
        compiler_params=_params("arbitrary", "arbitrary"),
        name="conv_mixer_big",
    )(xn, w_in, w_in, w_in, w_in, conv_w, ctx)


def conv_mixer_small(xn, w_in, conv_w, m1, m2, p1, p2, tn=256):
    rows, d = xn.shape
    dc = w_in.shape[1] // 4
    return pl.pallas_call(
        _conv_small_kernel,
        grid=(dc // tn, 1),
        in_specs=[pl.BlockSpec((rows, d), lambda j, i: (0, 0))] + _w_in_specs(d, dc, tn) + [
            pl.BlockSpec((CONV_WIDTH, tn), lambda j, i: (0, j)),
            pl.BlockSpec((rows, 1), lambda j, i: (0, 0)),
            pl.BlockSpec((rows, 1), lambda j, i: (0, 0)),
            pl.BlockSpec((rows, tn), lambda j, i: (0, j)),
            pl.BlockSpec((rows, tn), lambda j, i: (0, j))],
        out_specs=[pl.BlockSpec((rows, tn), lambda j, i: (0, j)),
                   pl.BlockSpec((rows, tn), lambda j, i: (0, j))],
        out_shape=[jax.ShapeDtypeStruct((rows, dc), BF16),
                   jax.ShapeDtypeStruct((rows, dc), F32)],
        compiler_params=_params("parallel", "arbitrary"),
        name="conv_mixer_small",
    )(xn, w_in, w_in, w_in, w_in, conv_w, m1, m2, p1, p2)


def _proj_residual_kernel(x_ref, w_ref, h_ref, g_ref, o_ref):
    a = _dot(x_ref[...], w_ref[...])
    inv = lax.rsqrt(jnp.mean(a * a, axis=-1, keepdims=True) + RMS_EPS)
    o_ref[...] = h_ref[...] + a * inv * g_ref[...]


def proj_residual(x, w, h, gain, tm=256):
    rows, kdim = x.shape
    d = w.shape[1]
    tm = _row_tile(rows, tm)
    return pl.pallas_call(
        _proj_residual_kernel,
        grid=(rows // tm,),
        in_specs=[pl.BlockSpec((tm, kdim), lambda i: (i, 0)),
                  pl.BlockSpec((kdim, d), lambda i: (0, 0), pipeline_mode=pl.Buffered(1)),
                  pl.BlockSpec((tm, d), lambda i: (i, 0)),
                  pl.BlockSpec((1, d), lambda i: (0, 0))],
        out_specs=pl.BlockSpec((tm, d), lambda i: (i, 0)),
        out_shape=jax.ShapeDtypeStruct((rows, d), F32),
        compiler_params=_params("arbitrary"),
        name="proj_residual",
    )(x, w, h, gain)


def _mm_kernel(x_ref, xs_ref, w_ref, o_ref, os_ref, *, epilogue):
    o_ref[...] = epilogue(_dot(x_ref[...], w_ref[...])).astype(o_ref.dtype)

    @pl.when(pl.program_id(1) == 0)
    def _():
        os_ref[...] = epilogue(_dot(xs_ref[...], w_ref[...])).astype(os_ref.dtype)


def matmul(x, xs, w, out_dtype, epilogue=lambda y: y, tm=1024, tn=1024):
    rows, kdim = x.shape
    rows_s = xs.shape[0]
    n = w.shape[1]
    tm = _row_tile(rows, tm)
    tn = min(tn, n)
    return pl.pallas_call(
        functools.partial(_mm_kernel, epilogue=epilogue),
        grid=(n // tn, rows // tm),
        in_specs=[pl.BlockSpec((tm, kdim), lambda j, i: (i, 0)),
                  pl.BlockSpec((rows_s, kdim), lambda j, i: (0, 0)),
                  pl.BlockSpec((kdim, tn), lambda j, i: (0, j))],
        out_specs=[pl.BlockSpec((tm, tn), lambda j, i: (i, j)),
                   pl.BlockSpec((rows_s, tn), lambda j, i: (0, j))],
        out_shape=[jax.ShapeDtypeStruct((rows, n), out_dtype),
                   jax.ShapeDtypeStruct((rows_s, n), out_dtype)],
        compiler_params=_params("parallel", "arbitrary"),
        name="matmul",
    )(x, xs, w)


def _silu(z):
    return z * _sigmoid(z)


def _forget_gate_kernel(x_ref, w_ref, b_ref, o_ref):
    t = _dot(x_ref[...], w_ref[...]) + b_ref[...]
    o_ref[...] = jnp.minimum(t, 0.0) - jnp.log(1.0 + jnp.exp(-jnp.abs(t)))


def forget_gate(x, w_f, b_f):
    rows, kdim = x.shape
    tm = _row_tile(rows, 1024)
    return pl.pallas_call(
        _forget_gate_kernel,
        grid=(rows // tm,),
        in_specs=[pl.BlockSpec((tm, kdim), lambda i: (i, 0)),
                  pl.BlockSpec((kdim, LANES), lambda i: (0, 0)),
                  pl.BlockSpec((1, LANES), lambda i: (0, 0))],
        out_specs=pl.BlockSpec((tm, LANES), lambda i: (i, 0)),
        out_shape=jax.ShapeDtypeStruct((rows, LANES), F32),
        compiler_params=_params("parallel"),
        name="forget_gate",
    )(x, w_f, b_f)


def _cumsum_kernel(x_ref, o_ref, *, n_chunks):
    r = lax.broadcasted_iota(jnp.int32, (LANES, LANES), 0)
    c = lax.broadcasted_iota(jnp.int32, (LANES, LANES), 1)
    tri = jnp.where(c <= r, 1.0, 0.0).astype(BF16)
    carry = jnp.zeros((1, LANES), F32)
    for n in range(n_chunks):
        hi, mid, lo = _split3(x_ref[0, n * LANES:(n + 1) * LANES, :])
        f = _dot(tri, hi) + _dot(tri, mid) + _dot(tri, lo) + carry
        o_ref[0, n * LANES:(n + 1) * LANES, :] = f
        carry = f[LANES - 1:LANES, :]


def cumsum_rows(x):
    b, length, lanes = x.shape
    return pl.pallas_call(
        functools.partial(_cumsum_kernel, n_chunks=length // LANES),
        grid=(b,),
        in_specs=[pl.BlockSpec((1, length, lanes), lambda i: (i, 0, 0))],
        out_specs=pl.BlockSpec((1, length, lanes), lambda i: (i, 0, 0)),
        out_shape=jax.ShapeDtypeStruct(x.shape, F32),
        compiler_params=_params("parallel"),
        name="cumsum_rows",
    )(x)


def _attn_prompt_kernel(q_ref, k_ref, v_ref, f_ref, fm_ref, frow_ref,
                        sz_ref, o_ref, kb_ref, vt_ref, fcol_ref, *, heads_per_step, n_meta, tq):
    hg = pl.program_id(1)
    lq = q_ref.shape[0]
    nq = lq // tq
    n_heads = f_ref.shape[2]
    pad = LANES - n_meta
    key = lax.broadcasted_iota(jnp.int32, (tq, tq), 0)
    qry = lax.broadcasted_iota(jnp.int32, (tq, tq), 1)
    pad_row = lax.broadcasted_iota(jnp.int32, (LANES, 1), 0) < pad

    for hh in range(heads_per_step):
        sl = slice(hh * HEAD_DIM, (hh + 1) * HEAD_DIM)
        head = hg * heads_per_step + hh
        zeros = jnp.zeros((pad, HEAD_DIM), F32)
        k_meta, v_meta = k_ref[0:n_meta, sl], v_ref[0:n_meta, sl]
        kb_ref[hh, 0:LANES, :] = jnp.concatenate([zeros, k_meta], axis=0).astype(BF16)
        kb_ref[hh, LANES:, :] = k_ref[n_meta:, sl].astype(BF16)
        vt_ref[hh, :, 0:LANES] = jnp.concatenate([zeros, v_meta], axis=0).T.astype(BF16)
        for c in range(nq):
            vt_ref[hh, :, LANES + c * tq:LANES + (c + 1) * tq] = (
                v_ref[n_meta + c * tq:n_meta + (c + 1) * tq, sl].T.astype(BF16))
        pick_m = lax.broadcasted_iota(jnp.int32, (LANES, n_heads), 1) == head
        f_meta = jnp.sum(jnp.where(pick_m, fm_ref[0], 0.0), axis=-1, keepdims=True)
        fcol_ref[hh, 0:LANES, :] = jnp.where(pad_row, -MASKED, f_meta)
        pick = lax.broadcasted_iota(jnp.int32, (lq, n_heads), 1) == head
        fcol_ref[hh, LANES:, :] = jnp.sum(jnp.where(pick, f_ref[0], 0.0), axis=-1, keepdims=True)

    for qi in range(nq):
        nk = LANES + (qi + 1) * tq
        rows = slice(qi * tq, (qi + 1) * tq)
        for hh in range(heads_per_step):
            sl = slice(hh * HEAD_DIM, (hh + 1) * HEAD_DIM)
            s = _dot_nt(kb_ref[hh, 0:nk, :], q_ref[rows, sl])
            s = s + (frow_ref[0, hh, qi] - fcol_ref[hh, 0:nk, :])
            s = jnp.concatenate(
                [s[:nk - tq], jnp.where(key <= qry, s[nk - tq:], MASKED)], axis=0)
            m = jnp.max(s, axis=0, keepdims=True)
            p = jnp.exp(s - m)
            l = jnp.sum(p, axis=0, keepdims=True)
            acc = _dot(vt_ref[hh, :, 0:nk], p.astype(BF16))
            gate = sz_ref[rows, sl].astype(F32)
            o_ref[rows, sl] = ((acc / l).T * gate).astype(o_ref.dtype)


def attention_prompt(q, k, v, f_nat, f_meta, f_rows, sz, *, n_prompts, n_meta,
                     heads_per_step=2, tq=256):
    rows, d = q.shape
    lq = rows // n_prompts
    n_heads = d // HEAD_DIM
    w = heads_per_step * HEAD_DIM
    big = pl.BlockSpec((lq, w), lambda b, h: (b, h))
    keys = pl.BlockSpec((n_meta + lq, w), lambda b, h: (b, h))
    return pl.pallas_call(
        functools.partial(_attn_prompt_kernel, heads_per_step=heads_per_step,
                          n_meta=n_meta, tq=tq),
        grid=(n_prompts, n_heads // heads_per_step),
        in_specs=[big, keys, keys,
                  pl.BlockSpec((1, lq, n_heads), lambda b, h: (b, 0, 0)),
                  pl.BlockSpec((1, LANES, n_heads), lambda b, h: (b, 0, 0)),
                  pl.BlockSpec((1, heads_per_step, lq // tq, 1, tq),
                               lambda b, h: (b, h, 0, 0, 0)),
                  big],
        out_specs=big,
        out_shape=jax.ShapeDtypeStruct((rows, d), BF16),
        scratch_shapes=[pltpu.VMEM((heads_per_step, LANES + lq, HEAD_DIM), BF16),
                        pltpu.VMEM((heads_per_step, HEAD_DIM, LANES + lq), BF16),
                        pltpu.VMEM((heads_per_step, LANES + lq, 1), F32)],
        compiler_params=_params("parallel", "parallel"),
        name="attention_prompt",
    )(q, k, v, f_nat, f_meta, f_rows, sz)


def _attn_decode_kernel(pt_ref, *refs, pages, n_chunks, n_new, n_heads):
    k_refs = refs[0:pages]
    v_refs = refs[pages:2 * pages]
    lf_refs = refs[2 * pages:3 * pages]
    (qbd_ref, kn_ref, vn_ref, lfn_ref, sz_ref, o_ref,
     kc_ref, vc_ref, acc_ref, m_ref, l_ref, crow_ref, tail_ref, s_ref) = refs[3 * pages:]
    c = pl.program_id(1)
    page = lf_refs[0].shape[1]
    ncol = n_heads * n_new
    qbd = qbd_ref[0]

    def to_col(x):
        return jnp.broadcast_to(x, (LANES, ncol)).T[:, 0:1]

    @pl.when(c == 0)
    def _():
        rows_new = lfn_ref.shape[1]
        r = lax.broadcasted_iota(jnp.int32, (rows_new, ncol), 0)
        qcol = lax.broadcasted_iota(jnp.int32, (rows_new, ncol), 1) & (n_new - 1)
        lfn = lfn_ref[0]
        cum = jnp.zeros((rows_new, ncol), F32)
        for j in range(n_new):
            cum = cum + jnp.where(r >= j, lfn[j:j + 1, :], 0.0)
        crow = jnp.sum(jnp.where(r == qcol, cum, 0.0), axis=0, keepdims=True)
        s = _dot(kn_ref[0], qbd) + (crow - cum)
        s = jnp.where(r <= qcol, s, MASKED)
        m = jnp.max(s, axis=0, keepdims=True)
        p = jnp.exp(s - m)
        m_ref[...] = m
        l_ref[...] = jnp.sum(p, axis=0, keepdims=True)
        acc_ref[...] = _dot(p.T.astype(BF16), vn_ref[0])
        crow_ref[...] = crow
        tail_ref[...] = jnp.zeros_like(tail_ref)
        s_ref[...] = jnp.full(s_ref.shape, MASKED, F32)

    def head_major(page_ref, dst_ref, p_i):
        heads = pltpu.einshape("khd->hkd", page_ref[0]).astype(BF16)
        for h in range(n_heads):
            dst_ref[p_i * page:(p_i + 1) * page, h * HEAD_DIM:(h + 1) * HEAD_DIM] = heads[h]

    s_prev = s_ref[...]
    m_old = m_ref[...]
    m_new = jnp.maximum(m_old, jnp.max(s_prev, axis=0, keepdims=True))
    alpha = jnp.exp(m_old - m_new)
    p = jnp.exp(s_prev - m_new)
    m_ref[...] = m_new
    l_ref[...] = alpha * l_ref[...] + jnp.sum(p, axis=0, keepdims=True)
    for p_i in range(pages):
        head_major(v_refs[p_i], vc_ref, p_i)
    acc_ref[...] = acc_ref[...] * to_col(alpha) + _dot(p.T.astype(BF16), vc_ref[...])

    ri = lax.broadcasted_iota(jnp.int32, (page, page), 0)
    ci = lax.broadcasted_iota(jnp.int32, (page, page), 1)
    upper = jnp.where(ci > ri, 1.0, 0.0).astype(BF16)
    er = lax.broadcasted_iota(jnp.int32, (LANES, ncol), 0)
    ec = lax.broadcasted_iota(jnp.int32, (LANES, ncol), 1)
    expand = jnp.where(er * n_new == (ec & ~(n_new - 1)), 1.0, 0.0).astype(BF16)
    tail = tail_ref[...]
    decay = [None] * pages
    for p_i in reversed(range(pages)):
        parts = [_dot(x, expand).astype(BF16) for x in _split3(lf_refs[p_i][0])]
        g = tail
        for e in parts:
            g = g + _dot(upper, e)
        decay[p_i] = g
        for e in parts:
            tail = tail + jnp.sum(e.astype(F32), axis=0, keepdims=True)
        head_major(k_refs[p_i], kc_ref, p_i)
    tail_ref[...] = tail
    s_ref[...] = _dot(kc_ref[...], qbd) + (crow_ref[...] + jnp.concatenate(decay, axis=0))

    @pl.when(c == n_chunks)
    def _():
        inv_l = to_col(1.0 / l_ref[...])
        outs = []
        for h in range(n_heads):
            blk = acc_ref[h * n_new:(h + 1) * n_new, h * HEAD_DIM:(h + 1) * HEAD_DIM]
            outs.append(blk * inv_l[h * n_new:(h + 1) * n_new, :])
        o = jnp.concatenate(outs, axis=1)
        o_ref[0] = (o * sz_ref[0].astype(F32)).astype(o_ref.dtype)


def attention_decode(page_table, cache_k, cache_v, cache_lf, qbd, k_new, v_new, lf_new, sz,
                     *, n_new, pages=4):
    nb, n_pages = page_table.shape
    _, page, n_heads, _ = cache_k.shape
    d = n_heads * HEAD_DIM
    ncol = n_heads * n_new
    n_chunks = n_pages // pages

    def paged(lag, *tail):
        zeros = (0,) * len(tail)

        def index(p, b, c, pt):
            chunk = jnp.clip(c - lag, 0, n_chunks - 1)
            return (pt[b, (n_chunks - 1 - chunk) * pages + p],) + zeros

        return [pl.BlockSpec((1,) + tail, functools.partial(index, p)) for p in range(pages)]

    per_seq = lambda shape: pl.BlockSpec((1,) + shape, lambda b, c, pt: (b, 0, 0))
    grid_spec = pltpu.PrefetchScalarGridSpec(
        num_scalar_prefetch=1,
        grid=(nb, n_chunks + 1),
        in_specs=paged(0, page, n_heads, HEAD_DIM) + paged(1, page, n_heads, HEAD_DIM)
        + paged(0, page, LANES) + [
            per_seq((d, ncol)), per_seq((LANES, d)), per_seq((LANES, d)),
            per_seq((LANES, ncol)), per_seq((n_new, d))],
        out_specs=per_seq((n_new, d)),
        scratch_shapes=[pltpu.VMEM((pages * page, d), BF16), pltpu.VMEM((pages * page, d), BF16),
                        pltpu.VMEM((ncol, d), F32), pltpu.VMEM((1, ncol), F32),
                        pltpu.VMEM((1, ncol), F32), pltpu.VMEM((1, ncol), F32),
                        pltpu.VMEM((1, ncol), F32), pltpu.VMEM((pages * page, ncol), F32)],
    )
    return pl.pallas_call(
        functools.partial(_attn_decode_kernel, pages=pages, n_chunks=n_chunks, n_new=n_new,
                          n_heads=n_heads),
        grid_spec=grid_spec,
        out_shape=jax.ShapeDtypeStruct((nb, n_new, d), BF16),
        compiler_params=_params("parallel", "arbitrary"),
        name="attention_decode",
    )(page_table, *([cache_k] * pages), *([cache_v] * pages), *([cache_lf] * pages),
      qbd, k_new, v_new, lf_new, sz)


def kernel(x_prompt, x_sample, state_conv, cache_k, cache_v, cache_logf, page_table,
           meta_tokens, a_pre_g, a_post_g, a_w_in, a_conv_w, a_w_out, kv_g, w_kvf, b_f,
           b_pre_g, b_post_g, b_w_qz, b_w_o):
    n_prompts, seq, d = x_prompt.shape
    n_dec, n_new, _ = x_sample.shape
    n_meta = meta_tokens.shape[0]
    d_attn = b_w_o.shape[1]
    n_heads = d_attn // HEAD_DIM
    dc = a_conv_w.shape[2]
    n_small = n_meta + n_dec * n_new
    scale = HEAD_DIM ** -0.5

    w_in = a_w_in[0].astype(BF16)
    w_out = a_w_out[0].astype(BF16)
    w_k = w_kvf[:, :d_attn].astype(BF16)
    w_v = w_kvf[:, d_attn:2 * d_attn].astype(BF16)
    w_f = jnp.pad(w_kvf[:, 2 * d_attn:], ((0, 0), (0, LANES - n_heads))).astype(BF16)
    bias_f = jnp.pad(b_f, (0, LANES - n_heads)).reshape(1, LANES)
    w_q = b_w_qz[0][:, :d_attn].astype(BF16)
    w_z = b_w_qz[0][:, d_attn:].astype(BF16)
    w_o = b_w_o[0].astype(BF16)

    x_big = x_prompt.reshape(n_prompts * seq, d)
    x_small = jnp.concatenate([meta_tokens, x_sample.reshape(n_dec * n_new, d)], axis=0)

    starts = jnp.arange(n_small)
    in_dec = starts >= n_meta
    pos = jnp.where(in_dec, (starts - n_meta) % n_new, starts)
    m1 = (pos >= 1).astype(F32).reshape(n_small, 1)
    m2 = (pos >= 2).astype(F32).reshape(n_small, 1)
    st = state_conv[0]
    p1 = jnp.zeros((n_dec, n_new, dc), F32).at[:, 0].set(st[:, 1])
    p2 = jnp.zeros((n_dec, n_new, dc), F32).at[:, 0].set(st[:, 0]).at[:, 1].set(st[:, 1])
    zeros_meta = jnp.zeros((n_meta, dc), F32)
    p1 = jnp.concatenate([zeros_meta, p1.reshape(n_dec * n_new, dc)], axis=0)
    p2 = jnp.concatenate([zeros_meta, p2.reshape(n_dec * n_new, dc)], axis=0)

    (xn_small,) = rms_norm_bf16(x_small, a_pre_g)
    g_small, u_small = conv_mixer_small(xn_small, w_in, a_conv_w[0], m1, m2, p1, p2)
    (xn_big,) = rms_norm_bf16(x_big, a_pre_g)
    ctx = jnp.zeros((CARRY_ROWS, dc), F32).at[CARRY_ROWS - 2:].set(u_small[n_meta - 2:n_meta])
    g_big, tails = conv_mixer_big(xn_big, w_in, a_conv_w[0], ctx, seq)
    h_small = proj_residual(g_small, w_out, x_small, a_post_g)
    h_big = proj_residual(g_big, w_out, x_big, a_post_g)

    state_conv_prompt = tails[None, :, CARRY_ROWS - 2:, :]
    u_dec = u_small[n_meta:].reshape(n_dec, n_new, dc)
    state_conv_sample = u_dec[None, :, n_new - 2:, :]

    gains = jnp.stack([kv_g, b_pre_g[0]], axis=0)
    hkv_small, hb_small = rms_norm_bf16(h_small, gains)
    hkv_big, hb_big = rms_norm_bf16(h_big, gains)

    total = n_meta + seq
    hkv_full = jnp.concatenate(
        [jnp.broadcast_to(hkv_small[None, :n_meta], (n_prompts, n_meta, d)),
         hkv_big.reshape(n_prompts, seq, d)], axis=1).reshape(n_prompts * total, d)
    k_full, k_small = matmul(hkv_full, hkv_small, w_k, F32)
    v_full, v_small = matmul(hkv_full, hkv_small, w_v, F32)
    lf_full = forget_gate(hkv_full, w_f, bias_f)
    lf_small = forget_gate(hkv_small, w_f, bias_f)
    q_big, q_small = matmul(hb_big, hb_small, w_q, BF16, epilogue=lambda y: y * scale)
    sz_big, sz_small = matmul(hb_big, hb_small, w_z, BF16, epilogue=_silu)

    lf_prompt = lf_full.reshape(n_prompts, total, LANES)
    padded = -(-total // LANES) * LANES
    f_prompt = cumsum_rows(jnp.pad(lf_prompt, ((0, 0), (0, padded - total), (0, 0))))

    tq = 256
    f_nat = f_prompt[:, n_meta:total, :n_heads]
    f_rows = jnp.transpose(f_nat, (0, 2, 1)).reshape(n_prompts, n_heads, seq // tq, 1, tq)
    f_meta = jnp.pad(f_prompt[:, :n_meta, :n_heads], ((0, 0), (LANES - n_meta, 0), (0, 0)))
    og_big = attention_prompt(q_big, k_full, v_full, f_nat, f_meta, f_rows, sz_big,
                              n_prompts=n_prompts, n_meta=n_meta, tq=tq)
    y_prompt = proj_residual(og_big, w_o, h_big, b_post_g).reshape(n_prompts, seq, d)

    n_rows_dec = n_dec * n_new
    q_dec = q_small[n_meta:].reshape(n_dec, n_new, n_heads, HEAD_DIM)
    eye = jnp.eye(n_heads, dtype=BF16)
    qbd = jnp.einsum('bqhd,hg->bhdgq', q_dec, eye).reshape(n_dec, d_attn, n_heads * n_new)
    pad_rows = ((0, 0), (0, LANES - n_new), (0, 0))
    k_new = jnp.pad(k_small[n_meta:].reshape(n_dec, n_new, d_attn).astype(BF16), pad_rows)
    v_new = jnp.pad(v_small[n_meta:].reshape(n_dec, n_new, d_attn).astype(BF16), pad_rows)
    lf_dec = lf_small[n_meta:, :n_heads].reshape(n_dec, n_new, n_heads)
    lf_new = jnp.pad(jnp.repeat(lf_dec, n_new, axis=2), pad_rows)
    pool, page = cache_k.shape[0], cache_k.shape[1]
    cache_lf = jnp.pad(cache_logf, ((0, 0), (0, 0), (0, LANES - n_heads)))
    og_dec = attention_decode(
        page_table, cache_k, cache_v, cache_lf, qbd, k_new, v_new, lf_new, sz_small[n_meta:].reshape(n_dec, n_new, d_attn),
        n_new=n_new)
    y_sample = proj_residual(og_dec.reshape(n_rows_dec, d_attn), w_o, h_small[n_meta:],
                             b_post_g).reshape(n_dec, n_new, d)

    k_prompt = k_full.reshape(n_prompts, total, n_heads, HEAD_DIM)
    v_prompt = v_full.reshape(n_prompts, total, n_heads, HEAD_DIM)
    logf_prompt = lf_prompt[:, :, :n_heads]
    k_sample = k_small[n_meta:].reshape(n_dec, n_new, n_heads, HEAD_DIM)
    v_sample = v_small[n_meta:].reshape(n_dec, n_new, n_heads, HEAD_DIM)
    logf_sample = lf_dec
    return (y_prompt, y_sample, state_conv_prompt, state_conv_sample,
            k_prompt, v_prompt, logf_prompt, k_sample, v_sample, logf_sample)
```

```python
import functools

import jax
import jax.numpy as jnp
from jax import lax
from jax.experimental import pallas as pl
from jax.experimental.pallas import tpu as pltpu

F32 = jnp.float32
BF16 = jnp.bfloat16

RMS_EPS = 1e-6
CONV_WIDTH = 3
HEAD_DIM = 128
LANES = 128
CARRY_ROWS = 8
MASKED = -1e30
VMEM_LIMIT_BYTES = 60 * 1024 * 1024


def _params(*sem):
    return pltpu.CompilerParams(dimension_semantics=sem, vmem_limit_bytes=VMEM_LIMIT_BYTES)


def _dot(a, b):
    return jnp.dot(a, b, preferred_element_type=F32)


def _dot_nt(a, b):
    return lax.dot_general(a, b, (((1,), (1,)), ((), ())), preferred_element_type=F32)


def _split3(x):
    hi = x.astype(BF16)
    r1 = x - hi.astype(F32)
    mid = r1.astype(BF16)
    lo = (r1 - mid.astype(F32)).astype(BF16)
    return hi, mid, lo


def _sigmoid(z):
    return 1.0 / (1.0 + jnp.exp(-z))


BF16_ROWS = 16


def _row_tile(rows, want):
    for t in range(min(want, rows) // BF16_ROWS * BF16_ROWS, 0, -BF16_ROWS):
        if rows % t == 0:
            return t
    return rows


def _norm_kernel(x_ref, g_ref, *o_refs):
    x = x_ref[...]
    xhat = x * lax.rsqrt(jnp.mean(x * x, axis=-1, keepdims=True) + RMS_EPS)
    for n, o_ref in enumerate(o_refs):
        o_ref[...] = (xhat * g_ref[n:n + 1, :]).astype(o_ref.dtype)


def rms_norm_bf16(x, gains):
    rows, d = x.shape
    n = gains.shape[0]
    tm = _row_tile(rows, 512)
    return pl.pallas_call(
        _norm_kernel,
        grid=(rows // tm,),
        in_specs=[pl.BlockSpec((tm, d), lambda i: (i, 0)),
                  pl.BlockSpec((n, d), lambda i: (0, 0))],
        out_specs=[pl.BlockSpec((tm, d), lambda i: (i, 0))] * n,
        out_shape=[jax.ShapeDtypeStruct((rows, d), BF16)] * n,
        compiler_params=_params("parallel"),
        name="rms_norm",
    )(x, gains)


def _gated_conv(x, wb_ref, wc_ref, wh_ref, wz_ref, cw_ref, um1_fix, um2_fix):
    u = _dot(x, wc_ref[...]) * _dot(x, wh_ref[...])
    um1 = um1_fix(pltpu.roll(u, 1, 0))
    um2 = um2_fix(pltpu.roll(u, 2, 0))
    cw = cw_ref[...]
    y = cw[0:1, :] * um2 + cw[1:2, :] * um1 + cw[2:3, :] * u
    z = _dot(x, wz_ref[...])
    g = _dot(x, wb_ref[...]) * y * (z * _sigmoid(z))
    return u, g


def _conv_big_kernel(x_ref, wb_ref, wc_ref, wh_ref, wz_ref, cw_ref, ctx_ref,
                     g_ref, tail_ref, carry_ref, *, tiles_per_seq):
    i = pl.program_id(1)

    @pl.when(i % tiles_per_seq == 0)
    def _():
        carry_ref[...] = ctx_ref[...]

    carry = carry_ref[...]
    prev1 = carry[CARRY_ROWS - 1:CARRY_ROWS, :]
    prev2 = carry[CARRY_ROWS - 2:CARRY_ROWS - 1, :]
    tm, tn = g_ref.shape
    row = lax.broadcasted_iota(jnp.int32, (tm, tn), 0)
    u, g = _gated_conv(
        x_ref[...], wb_ref, wc_ref, wh_ref, wz_ref, cw_ref,
        lambda r1: jnp.where(row == 0, prev1, r1),
        lambda r2: jnp.where(row == 0, prev2, jnp.where(row == 1, prev1, r2)))
    tail = u[tm - CARRY_ROWS:, :]
    carry_ref[...] = tail
    tail_ref[0] = tail
    g_ref[...] = g.astype(g_ref.dtype)


def _conv_small_kernel(x_ref, wb_ref, wc_ref, wh_ref, wz_ref, cw_ref,
                       m1_ref, m2_ref, p1_ref, p2_ref, g_ref, u_ref):
    u, g = _gated_conv(
        x_ref[...], wb_ref, wc_ref, wh_ref, wz_ref, cw_ref,
        lambda r1: r1 * m1_ref[...] + p1_ref[...],
        lambda r2: r2 * m2_ref[...] + p2_ref[...])
    u_ref[...] = u
    g_ref[...] = g.astype(g_ref.dtype)


def _w_in_specs(d, dc, tn):
    nj = dc // tn
    return [pl.BlockSpec((d, tn), functools.partial(lambda c, j, i: (0, c * nj + j), c))
            for c in range(4)]


def conv_mixer_big(xn, w_in, conv_w, ctx, seq_len, tm=1024, tn=256):
    rows, d = xn.shape
    dc = w_in.shape[1] // 4
    tm = min(tm, seq_len)
    tps = seq_len // tm
    n_seq = rows // seq_len
    return pl.pallas_call(
        functools.partial(_conv_big_kernel, tiles_per_seq=tps),
        grid=(dc // tn, rows // tm),
        in_specs=[pl.BlockSpec((tm, d), lambda j, i: (i, 0))] + _w_in_specs(d, dc, tn) + [
            pl.BlockSpec((CONV_WIDTH, tn), lambda j, i: (0, j)),
            pl.BlockSpec((CARRY_ROWS, tn), lambda j, i: (0, j))],
        out_specs=[pl.BlockSpec((tm, tn), lambda j, i: (i, j)),
                   pl.BlockSpec((1, CARRY_ROWS, tn), lambda j, i: (i // tps, 0, j))],
        out_shape=[jax.ShapeDtypeStruct((rows, dc), BF16),
                   jax.ShapeDtypeStruct((n_seq, CARRY_ROWS, dc), F32)],
        scratch_shapes=[pltpu.VMEM((CARRY_ROWS, tn), F32)],
        compiler_params=_params("arbitrary", "arbitrary"),
        name="conv_mixer_big",
    )(xn, w_in, w_in, w_in, w_in, conv_w, ctx)


def conv_mixer_small(xn, w_in, conv_w, m1, m2, p1, p2, tn=256):
    rows, d = xn.shape
    dc = w_in.shape[1] // 4
    return pl.pallas_call(
        _conv_small_kernel,
        grid=(dc // tn, 1),
        in_specs=[pl.BlockSpec((rows, d), lambda j, i: (0, 0))] + _w_in_specs(d, dc, tn) + [
            pl.BlockSpec((CONV_WIDTH, tn), lambda j, i: (0, j)),
            pl.BlockSpec((rows, 1), lambda j, i: (0, 0)),
            pl.BlockSpec((rows, 1), lambda j, i: (0, 0)),
            pl.BlockSpec((rows, tn), lambda j, i: (0, j)),
            pl.BlockSpec((rows, tn), lambda j, i: (0, j))],
        out_specs=[pl.BlockSpec((rows, tn), lambda j, i: (0, j)),
                   pl.BlockSpec((rows, tn), lambda j, i: (0, j))],
        out_shape=[jax.ShapeDtypeStruct((rows, dc), BF16),
                   jax.ShapeDtypeStruct((rows, dc), F32)],
        compiler_params=_params("parallel", "arbitrary"),
        name="conv_mixer_small",
    )(xn, w_in, w_in, w_in, w_in, conv_w, m1, m2, p1, p2)


def _proj_residual_kernel(x_ref, w_ref, h_ref, g_ref, o_ref):
    a = _dot(x_ref[...], w_ref[...])
    inv = lax.rsqrt(jnp.mean(a * a, axis=-1, keepdims=True) + RMS_EPS)
    o_ref[...] = h_ref[...] + a * inv * g_ref[...]


def proj_residual(x, w, h, gain, tm=256):
    rows, kdim = x.shape
    d = w.shape[1]
    tm = _row_tile(rows, tm)
    return pl.pallas_call(
        _proj_residual_kernel,
        grid=(rows // tm,),
        in_specs=[pl.BlockSpec((tm, kdim), lambda i: (i, 0)),
                  pl.BlockSpec((kdim, d), lambda i: (0, 0), pipeline_mode=pl.Buffered(1)),
                  pl.BlockSpec((tm, d), lambda i: (i, 0)),
                  pl.BlockSpec((1, d), lambda i: (0, 0))],
        out_specs=pl.BlockSpec((tm, d), lambda i: (i, 0)),
        out_shape=jax.ShapeDtypeStruct((rows, d), F32),
        compiler_params=_params("arbitrary"),
        name="proj_residual",
    )(x, w, h, gain)


def _mm_kernel(x_ref, xs_ref, w_ref, o_ref, os_ref, wb_ref, *, epilogue):
    @pl.when(pl.program_id(1) == 0)
    def _():
        wb_ref[...] = w_ref[...].astype(BF16)
        os_ref[...] = epilogue(_dot(xs_ref[...], wb_ref[...])).astype(os_ref.dtype)

    o_ref[...] = epilogue(_dot(x_ref[...], wb_ref[...])).astype(o_ref.dtype)


def matmul(x, xs, w, col0, n, out_dtype, epilogue=lambda y: y, tm=1024, tn=512):
    rows, kdim = x.shape
    rows_s = xs.shape[0]
    tm = _row_tile(rows, tm)
    tn = min(tn, n)
    j0 = col0 // tn
    return pl.pallas_call(
        functools.partial(_mm_kernel, epilogue=epilogue),
        grid=(n // tn, rows // tm),
        in_specs=[pl.BlockSpec((tm, kdim), lambda j, i: (i, 0)),
                  pl.BlockSpec((rows_s, kdim), lambda j, i: (0, 0)),
                  pl.BlockSpec((kdim, tn), lambda j, i: (0, j0 + j))],
        out_specs=[pl.BlockSpec((tm, tn), lambda j, i: (i, j)),
                   pl.BlockSpec((rows_s, tn), lambda j, i: (0, j))],
        out_shape=[jax.ShapeDtypeStruct((rows, n), out_dtype),
                   jax.ShapeDtypeStruct((rows_s, n), out_dtype)],
        scratch_shapes=[pltpu.VMEM((kdim, tn), BF16)],
        compiler_params=_params("parallel", "arbitrary"),
        name="matmul",
    )(x, xs, w)


def _silu(z):
    return z * _sigmoid(z)


def _forget_gate_kernel(x_ref, w_ref, b_ref, o_ref):
    t = _dot(x_ref[...], w_ref[...]) + b_ref[...]
    o_ref[...] = jnp.minimum(t, 0.0) - jnp.log(1.0 + jnp.exp(-jnp.abs(t)))


def forget_gate(x, w_f, b_f):
    rows, kdim = x.shape
    tm = _row_tile(rows, 1024)
    return pl.pallas_call(
        _forget_gate_kernel,
        grid=(rows // tm,),
        in_specs=[pl.BlockSpec((tm, kdim), lambda i: (i, 0)),
                  pl.BlockSpec((kdim, LANES), lambda i: (0, 0)),
                  pl.BlockSpec((1, LANES), lambda i: (0, 0))],
        out_specs=pl.BlockSpec((tm, LANES), lambda i: (i, 0)),
        out_shape=jax.ShapeDtypeStruct((rows, LANES), F32),
        compiler_params=_params("parallel"),
        name="forget_gate",
    )(x, w_f, b_f)


def _cumsum_kernel(x_ref, o_ref, *, n_chunks):
    r = lax.broadcasted_iota(jnp.int32, (LANES, LANES), 0)
    c = lax.broadcasted_iota(jnp.int32, (LANES, LANES), 1)
    tri = jnp.where(c <= r, 1.0, 0.0).astype(BF16)
    carry = jnp.zeros((1, LANES), F32)
    for n in range(n_chunks):
        hi, mid, lo = _split3(x_ref[0, n * LANES:(n + 1) * LANES, :])
        f = _dot(tri, hi) + _dot(tri, mid) + _dot(tri, lo) + carry
        o_ref[0, n * LANES:(n + 1) * LANES, :] = f
        carry = f[LANES - 1:LANES, :]


def cumsum_rows(x):
    b, length, lanes = x.shape
    return pl.pallas_call(
        functools.partial(_cumsum_kernel, n_chunks=length // LANES),
        grid=(b,),
        in_specs=[pl.BlockSpec((1, length, lanes), lambda i: (i, 0, 0))],
        out_specs=pl.BlockSpec((1, length, lanes), lambda i: (i, 0, 0)),
        out_shape=jax.ShapeDtypeStruct(x.shape, F32),
        compiler_params=_params("parallel"),
        name="cumsum_rows",
    )(x)


def _attn_prompt_kernel(q_ref, k_ref, v_ref, f_ref, fm_ref, frow_ref,
                        sz_ref, o_ref, kb_ref, vt_ref, fcol_ref, s_ref, *, heads_per_step,
                        n_meta, tq):
    hg = pl.program_id(1)
    lq = q_ref.shape[0]
    nq = lq // tq
    n_heads = f_ref.shape[2]
    pad = LANES - n_meta
    key = lax.broadcasted_iota(jnp.int32, (tq, tq), 0)
    qry = lax.broadcasted_iota(jnp.int32, (tq, tq), 1)
    pad_row = lax.broadcasted_iota(jnp.int32, (LANES, 1), 0) < pad

    for hh in range(heads_per_step):
        sl = slice(hh * HEAD_DIM, (hh + 1) * HEAD_DIM)
        head = hg * heads_per_step + hh
        zeros = jnp.zeros((pad, HEAD_DIM), F32)
        k_meta, v_meta = k_ref[0:n_meta, sl], v_ref[0:n_meta, sl]
        kb_ref[hh, 0:LANES, :] = jnp.concatenate([zeros, k_meta], axis=0).astype(BF16)
        kb_ref[hh, LANES:, :] = k_ref[n_meta:, sl].astype(BF16)
        vt_ref[hh, :, 0:LANES] = jnp.concatenate([zeros, v_meta], axis=0).T.astype(BF16)
        for c in range(nq):
            vt_ref[hh, :, LANES + c * tq:LANES + (c + 1) * tq] = (
                v_ref[n_meta + c * tq:n_meta + (c + 1) * tq, sl].T.astype(BF16))
        pick_m = lax.broadcasted_iota(jnp.int32, (LANES, n_heads), 1) == head
        f_meta = jnp.sum(jnp.where(pick_m, fm_ref[0], 0.0), axis=-1, keepdims=True)
        fcol_ref[hh, 0:LANES, :] = jnp.where(pad_row, -MASKED, f_meta)
        pick = lax.broadcasted_iota(jnp.int32, (lq, n_heads), 1) == head
        fcol_ref[hh, LANES:, :] = jnp.sum(jnp.where(pick, f_ref[0], 0.0), axis=-1, keepdims=True)

    for qi in range(nq):
        rows = slice(qi * tq, (qi + 1) * tq)
        tiles = [(0, LANES)] + [(LANES + j * tq, LANES + (j + 1) * tq) for j in range(qi + 1)]
        for hh in range(heads_per_step):
            sl = slice(hh * HEAD_DIM, (hh + 1) * HEAD_DIM)
            buf = (qi * heads_per_step + hh) % s_ref.shape[0]
            q = q_ref[rows, sl]
            f_row = frow_ref[0, hh, qi]
            m = None
            for a, b in tiles:
                s = _dot_nt(kb_ref[hh, a:b, :], q) + (f_row - fcol_ref[hh, a:b, :])
                if b == tiles[-1][1]:
                    s = jnp.where(key <= qry, s, MASKED)
                s_ref[buf, a:b, :] = s
                tile_max = jnp.max(s, axis=0, keepdims=True)
                m = tile_max if m is None else jnp.maximum(m, tile_max)
            l = jnp.zeros((1, tq), F32)
            acc = jnp.zeros((HEAD_DIM, tq), F32)
            for a, b in tiles:
                p = jnp.exp(s_ref[buf, a:b, :] - m)
                l = l + jnp.sum(p, axis=0, keepdims=True)
                acc = acc + _dot(vt_ref[hh, :, a:b], p.astype(BF16))
            gate = sz_ref[rows, sl].astype(F32)
            o_ref[rows, sl] = ((acc / l).T * gate).astype(o_ref.dtype)


def attention_prompt(q, k, v, f_nat, f_meta, f_rows, sz, *, n_prompts, n_meta,
                     heads_per_step=1, tq=256):
    rows, d = q.shape
    lq = rows // n_prompts
    n_heads = d // HEAD_DIM
    w = heads_per_step * HEAD_DIM
    big = pl.BlockSpec((lq, w), lambda b, h: (b, h))
    keys = pl.BlockSpec((n_meta + lq, w), lambda b, h: (b, h))
    return pl.pallas_call(
        functools.partial(_attn_prompt_kernel, heads_per_step=heads_per_step,
                          n_meta=n_meta, tq=tq),
        grid=(n_prompts, n_heads // heads_per_step),
        in_specs=[big, keys, keys,
                  pl.BlockSpec((1, lq, n_heads), lambda b, h: (b, 0, 0)),
                  pl.BlockSpec((1, LANES, n_heads), lambda b, h: (b, 0, 0)),
                  pl.BlockSpec((1, heads_per_step, lq // tq, 1, tq),
                               lambda b, h: (b, h, 0, 0, 0)),
                  big],
        out_specs=big,
        out_shape=jax.ShapeDtypeStruct((rows, d), BF16),
        scratch_shapes=[pltpu.VMEM((heads_per_step, LANES + lq, HEAD_DIM), BF16),
                        pltpu.VMEM((heads_per_step, HEAD_DIM, LANES + lq), BF16),
                        pltpu.VMEM((heads_per_step, LANES + lq, 1), F32),
                        pltpu.VMEM((2, LANES + lq, tq), F32)],
        compiler_params=_params("parallel", "parallel"),
        name="attention_prompt",
    )(q, k, v, f_nat, f_meta, f_rows, sz)


def _attn_decode_kernel(pt_ref, *refs, pages, n_chunks, n_new, n_heads):
    k_refs = refs[0:pages]
    v_refs = refs[pages:2 * pages]
    lf_refs = refs[2 * pages:3 * pages]
    (qbd_ref, kn_ref, vn_ref, lfn_ref, sz_ref, o_ref,
     kc_ref, vc_ref, acc_ref, m_ref, l_ref, crow_ref, tail_ref, s_ref) = refs[3 * pages:]
    c = pl.program_id(1)
    page = lf_refs[0].shape[1]
    ncol = n_heads * n_new
    qbd = qbd_ref[0]

    def to_col(x):
        return jnp.broadcast_to(x, (LANES, ncol)).T[:, 0:1]

    @pl.when(c == 0)
    def _():
        rows_new = lfn_ref.shape[1]
        r = lax.broadcasted_iota(jnp.int32, (rows_new, ncol), 0)
        qcol = lax.broadcasted_iota(jnp.int32, (rows_new, ncol), 1) & (n_new - 1)
        lfn = lfn_ref[0]
        cum = jnp.zeros((rows_new, ncol), F32)
        for j in range(n_new):
            cum = cum + jnp.where(r >= j, lfn[j:j + 1, :], 0.0)
        crow = jnp.sum(jnp.where(r == qcol, cum, 0.0), axis=0, keepdims=True)
        s = _dot(kn_ref[0], qbd) + (crow - cum)
        s = jnp.where(r <= qcol, s, MASKED)
        m = jnp.max(s, axis=0, keepdims=True)
        p = jnp.exp(s - m)
        m_ref[...] = m
        l_ref[...] = jnp.sum(p, axis=0, keepdims=True)
        acc_ref[...] = _dot(p.T.astype(BF16), vn_ref[0])
        crow_ref[...] = crow
        tail_ref[...] = jnp.zeros_like(tail_ref)
        s_ref[...] = jnp.full(s_ref.shape, MASKED, F32)

    def head_major(page_ref, dst_ref, p_i):
        heads = jnp.swapaxes(page_ref[0], 0, 1).astype(BF16)
        for h in range(n_heads):
            dst_ref[p_i * page:(p_i + 1) * page, h * HEAD_DIM:(h + 1) * HEAD_DIM] = heads[h]

    s_prev = s_ref[...]
    m_old = m_ref[...]
    m_new = jnp.maximum(m_old, jnp.max(s_prev, axis=0, keepdims=True))
    alpha = jnp.exp(m_old - m_new)
    p = jnp.exp(s_prev - m_new)
    m_ref[...] = m_new
    l_ref[...] = alpha * l_ref[...] + jnp.sum(p, axis=0, keepdims=True)
    for p_i in range(pages):
        head_major(v_refs[p_i], vc_ref, p_i)
    acc_ref[...] = acc_ref[...] * to_col(alpha) + _dot(p.T.astype(BF16), vc_ref[...])

    ri = lax.broadcasted_iota(jnp.int32, (page, page), 0)
    ci = lax.broadcasted_iota(jnp.int32, (page, page), 1)
    upper = jnp.where(ci > ri, 1.0, 0.0).astype(BF16)
    er = lax.broadcasted_iota(jnp.int32, (LANES, ncol), 0)
    ec = lax.broadcasted_iota(jnp.int32, (LANES, ncol), 1)
    expand = jnp.where(er * n_new == (ec & ~(n_new - 1)), 1.0, 0.0).astype(BF16)
    tail = tail_ref[...]
    decay = [None] * pages
    for p_i in reversed(range(pages)):
        parts = [_dot(x, expand).astype(BF16) for x in _split3(lf_refs[p_i][0])]
        g = tail
        for e in parts:
            g = g + _dot(upper, e)
        decay[p_i] = g
        for e in parts:
            tail = tail + jnp.sum(e.astype(F32), axis=0, keepdims=True)
        head_major(k_refs[p_i], kc_ref, p_i)
    tail_ref[...] = tail
    s_ref[...] = _dot(kc_ref[...], qbd) + (crow_ref[...] + jnp.concatenate(decay, axis=0))

    @pl.when(c == n_chunks)
    def _():
        inv_l = to_col(1.0 / l_ref[...])
        outs = []
        for h in range(n_heads):
            blk = acc_ref[h * n_new:(h + 1) * n_new, h * HEAD_DIM:(h + 1) * HEAD_DIM]
            outs.append(blk * inv_l[h * n_new:(h + 1) * n_new, :])
        o = jnp.concatenate(outs, axis=1)
        o_ref[0] = (o * sz_ref[0].astype(F32)).astype(o_ref.dtype)


def attention_decode(page_table, cache_k, cache_v, cache_lf, qbd, k_new, v_new, lf_new, sz,
                     *, n_new, pages=4):
    nb, n_pages = page_table.shape
    _, page, n_heads, _ = cache_k.shape
    d = n_heads * HEAD_DIM
    ncol = n_heads * n_new
    n_chunks = n_pages // pages

    def paged(lag, *tail):
        zeros = (0,) * len(tail)

        def index(p, b, c, pt):
            chunk = jnp.clip(c - lag, 0, n_chunks - 1)
            return (pt[b, (n_chunks - 1 - chunk) * pages + p],) + zeros

        return [pl.BlockSpec((1,) + tail, functools.partial(index, p)) for p in range(pages)]

    per_seq = lambda shape: pl.BlockSpec((1,) + shape, lambda b, c, pt: (b, 0, 0))
    grid_spec = pltpu.PrefetchScalarGridSpec(
        num_scalar_prefetch=1,
        grid=(nb, n_chunks + 1),
        in_specs=paged(0, page, n_heads, HEAD_DIM) + paged(1, page, n_heads, HEAD_DIM)
        + paged(0, page, LANES) + [
            per_seq((d, ncol)), per_seq((LANES, d)), per_seq((LANES, d)),
            per_seq((LANES, ncol)), per_seq((n_new, d))],
        out_specs=per_seq((n_new, d)),
        scratch_shapes=[pltpu.VMEM((pages * page, d), BF16), pltpu.VMEM((pages * page, d), BF16),
                        pltpu.VMEM((ncol, d), F32), pltpu.VMEM((1, ncol), F32),
                        pltpu.VMEM((1, ncol), F32), pltpu.VMEM((1, ncol), F32),
                        pltpu.VMEM((1, ncol), F32), pltpu.VMEM((pages * page, ncol), F32)],
    )
    return pl.pallas_call(
        functools.partial(_attn_decode_kernel, pages=pages, n_chunks=n_chunks, n_new=n_new,
                          n_heads=n_heads),
        grid_spec=grid_spec,
        out_shape=jax.ShapeDtypeStruct((nb, n_new, d), BF16),
        compiler_params=_params("parallel", "arbitrary"),
        name="attention_decode",
    )(page_table, *([cache_k] * pages), *([cache_v] * pages), *([cache_lf] * pages),
      qbd, k_new, v_new, lf_new, sz)


def kernel(x_prompt, x_sample, state_conv, cache_k, cache_v, cache_logf, page_table,
           meta_tokens, a_pre_g, a_post_g, a_w_in, a_conv_w, a_w_out, kv_g, w_kvf, b_f,
           b_pre_g, b_post_g, b_w_qz, b_w_o):
    n_prompts, seq, d = x_prompt.shape
    n_dec, n_new, _ = x_sample.shape
    n_meta = meta_tokens.shape[0]
    d_attn = b_w_o.shape[1]
    n_heads = d_attn // HEAD_DIM
    dc = a_conv_w.shape[2]
    n_small = n_meta + n_dec * n_new
    scale = HEAD_DIM ** -0.5

    w_in = a_w_in[0].astype(BF16)
    w_out = a_w_out[0].astype(BF16)
    w_f = jnp.pad(w_kvf[:, 2 * d_attn:], ((0, 0), (0, LANES - n_heads))).astype(BF16)
    bias_f = jnp.pad(b_f, (0, LANES - n_heads)).reshape(1, LANES)
    w_qz = b_w_qz[0]
    w_o = b_w_o[0].astype(BF16)

    x_big = x_prompt.reshape(n_prompts * seq, d)
    x_small = jnp.concatenate([meta_tokens, x_sample.reshape(n_dec * n_new, d)], axis=0)

    starts = jnp.arange(n_small)
    in_dec = starts >= n_meta
    pos = jnp.where(in_dec, (starts - n_meta) % n_new, starts)
    m1 = (pos >= 1).astype(F32).reshape(n_small, 1)
    m2 = (pos >= 2).astype(F32).reshape(n_small, 1)
    st = state_conv[0]
    p1 = jnp.zeros((n_dec, n_new, dc), F32).at[:, 0].set(st[:, 1])
    p2 = jnp.zeros((n_dec, n_new, dc), F32).at[:, 0].set(st[:, 0]).at[:, 1].set(st[:, 1])
    zeros_meta = jnp.zeros((n_meta, dc), F32)
    p1 = jnp.concatenate([zeros_meta, p1.reshape(n_dec * n_new, dc)], axis=0)
    p2 = jnp.concatenate([zeros_meta, p2.reshape(n_dec * n_new, dc)], axis=0)

    (xn_small,) = rms_norm_bf16(x_small, a_pre_g)
    g_small, u_small = conv_mixer_small(xn_small, w_in, a_conv_w[0], m1, m2, p1, p2)
    (xn_big,) = rms_norm_bf16(x_big, a_pre_g)
    ctx = jnp.zeros((CARRY_ROWS, dc), F32).at[CARRY_ROWS - 2:].set(u_small[n_meta - 2:n_meta])
    g_big, tails = conv_mixer_big(xn_big, w_in, a_conv_w[0], ctx, seq)
    h_small = proj_residual(g_small, w_out, x_small, a_post_g)
    h_big = proj_residual(g_big, w_out, x_big, a_post_g)

    state_conv_prompt = tails[None, :, CARRY_ROWS - 2:, :]
    u_dec = u_small[n_meta:].reshape(n_dec, n_new, dc)
    state_conv_sample = u_dec[None, :, n_new - 2:, :]

    gains = jnp.stack([kv_g, b_pre_g[0]], axis=0)
    hkv_small, hb_small = rms_norm_bf16(h_small, gains)
    hkv_big, hb_big = rms_norm_bf16(h_big, gains)

    total = n_meta + seq
    hkv_full = jnp.concatenate(
        [jnp.broadcast_to(hkv_small[None, :n_meta], (n_prompts, n_meta, d)),
         hkv_big.reshape(n_prompts, seq, d)], axis=1).reshape(n_prompts * total, d)
    k_full, k_small = matmul(hkv_full, hkv_small, w_kvf, 0, d_attn, F32)
    v_full, v_small = matmul(hkv_full, hkv_small, w_kvf, d_attn, d_attn, F32)
    lf_full = forget_gate(hkv_full, w_f, bias_f)
    lf_small = forget_gate(hkv_small, w_f, bias_f)
    q_big, q_small = matmul(hb_big, hb_small, w_qz, 0, d_attn, BF16,
                            epilogue=lambda y: y * scale)
    sz_big, sz_small = matmul(hb_big, hb_small, w_qz, d_attn, d_attn, BF16, epilogue=_silu)

    lf_prompt = lf_full.reshape(n_prompts, total, LANES)
    padded = -(-total // LANES) * LANES
    f_prompt = cumsum_rows(jnp.pad(lf_prompt, ((0, 0), (0, padded - total), (0, 0))))

    tq = 256
    f_nat = f_prompt[:, n_meta:total, :n_heads]
    f_rows = jnp.transpose(f_nat, (0, 2, 1)).reshape(n_prompts, n_heads, seq // tq, 1, tq)
    f_meta = jnp.pad(f_prompt[:, :n_meta, :n_heads], ((0, 0), (LANES - n_meta, 0), (0, 0)))
    og_big = attention_prompt(q_big, k_full, v_full, f_nat, f_meta, f_rows, sz_big,
                              n_prompts=n_prompts, n_meta=n_meta, tq=tq)
    y_prompt = proj_residual(og_big, w_o, h_big, b_post_g).reshape(n_prompts, seq, d)

    n_rows_dec = n_dec * n_new
    q_dec = q_small[n_meta:].reshape(n_dec, n_new, n_heads, HEAD_DIM)
    eye = jnp.eye(n_heads, dtype=BF16)
    qbd = jnp.einsum('bqhd,hg->bhdgq', q_dec, eye).reshape(n_dec, d_attn, n_heads * n_new)
    pad_rows = ((0, 0), (0, LANES - n_new), (0, 0))
    k_new = jnp.pad(k_small[n_meta:].reshape(n_dec, n_new, d_attn).astype(BF16), pad_rows)
    v_new = jnp.pad(v_small[n_meta:].reshape(n_dec, n_new, d_attn).astype(BF16), pad_rows)
    lf_dec = lf_small[n_meta:, :n_heads].reshape(n_dec, n_new, n_heads)
    lf_new = jnp.pad(jnp.repeat(lf_dec, n_new, axis=2), pad_rows)
    pool, page = cache_k.shape[0], cache_k.shape[1]
    cache_lf = jnp.pad(cache_logf, ((0, 0), (0, 0), (0, LANES - n_heads)))
    og_dec = attention_decode(
        page_table, cache_k, cache_v, cache_lf, qbd, k_new, v_new, lf_new, sz_small[n_meta:].reshape(n_dec, n_new, d_attn),
        n_new=n_new)
    y_sample = proj_residual(og_dec.reshape(n_rows_dec, d_attn), w_o, h_small[n_meta:],
                             b_post_g).reshape(n_dec, n_new, d)

    k_prompt = k_full.reshape(n_prompts, total, n_heads, HEAD_DIM)
    v_prompt = v_full.reshape(n_prompts, total, n_heads, HEAD_DIM)
    logf_prompt = lf_prompt[:, :, :n_heads]
    k_sample = k_small[n_meta:].reshape(n_dec, n_new, n_heads, HEAD_DIM)
    v_sample = v_small[n_meta:].reshape(n_dec, n_new, n_heads, HEAD_DIM)
    logf_sample = lf_dec
    return (y_prompt, y_sample, state_conv_prompt, state_conv_sample,
            k_prompt, v_prompt, logf_prompt, k_sample, v_sample, logf_sample)
```

```python
import functools

import jax
import jax.numpy as jnp
from jax import lax
from jax.experimental import pallas as pl
from jax.experimental.pallas import tpu as pltpu

F32 = jnp.float32
BF16 = jnp.bfloat16

RMS_EPS = 1e-6
CONV_WIDTH = 3
HEAD_DIM = 128
LANES = 128
CARRY_ROWS = 8
MASKED = -1e30
VMEM_LIMIT_BYTES = 60 * 1024 * 1024


def _params(*sem):
    return pltpu.CompilerParams(dimension_semantics=sem, vmem_limit_bytes=VMEM_LIMIT_BYTES)


def _dot(a, b):
    return jnp.dot(a, b, preferred_element_type=F32)


def _dot_nt(a, b):
    return lax.dot_general(a, b, (((1,), (1,)), ((), ())), preferred_element_type=F32)


def _split3(x):
    hi = x.astype(BF16)
    r1 = x - hi.astype(F32)
    mid = r1.astype(BF16)
    lo = (r1 - mid.astype(F32)).astype(BF16)
    return hi, mid, lo


def _sigmoid(z):
    return 1.0 / (1.0 + jnp.exp(-z))


BF16_ROWS = 16


def _row_tile(rows, want):
    for t in range(min(want, rows) // BF16_ROWS * BF16_ROWS, 0, -BF16_ROWS):
        if rows % t == 0:
            return t
    return rows


def _norm_kernel(x_ref, g_ref, *o_refs):
    x = x_ref[...]
    xhat = x * lax.rsqrt(jnp.mean(x * x, axis=-1, keepdims=True) + RMS_EPS)
    for n, o_ref in enumerate(o_refs):
        o_ref[...] = (xhat * g_ref[n:n + 1, :]).astype(o_ref.dtype)


def rms_norm_bf16(x, gains):
    rows, d = x.shape
    n = gains.shape[0]
    tm = _row_tile(rows, 512)
    return pl.pallas_call(
        _norm_kernel,
        grid=(rows // tm,),
        in_specs=[pl.BlockSpec((tm, d), lambda i: (i, 0)),
                  pl.BlockSpec((n, d), lambda i: (0, 0))],
        out_specs=[pl.BlockSpec((tm, d), lambda i: (i, 0))] * n,
        out_shape=[jax.ShapeDtypeStruct((rows, d), BF16)] * n,
        compiler_params=_params("parallel"),
        name="rms_norm",
    )(x, gains)


def _gated_conv(x, wb_ref, wc_ref, wh_ref, wz_ref, cw_ref, um1_fix, um2_fix):
    u = _dot(x, wc_ref[...]) * _dot(x, wh_ref[...])
    um1 = um1_fix(pltpu.roll(u, 1, 0))
    um2 = um2_fix(pltpu.roll(u, 2, 0))
    cw = cw_ref[...]
    y = cw[0:1, :] * um2 + cw[1:2, :] * um1 + cw[2:3, :] * u
    z = _dot(x, wz_ref[...])
    g = _dot(x, wb_ref[...]) * y * (z * _sigmoid(z))
    return u, g


def _conv_big_kernel(x_ref, wb_ref, wc_ref, wh_ref, wz_ref, cw_ref, ctx_ref,
                     g_ref, tail_ref, carry_ref, *, tiles_per_seq):
    i = pl.program_id(1)

    @pl.when(i % tiles_per_seq == 0)
    def _():
        carry_ref[...] = ctx_ref[...]

    carry = carry_ref[...]
    prev1 = carry[CARRY_ROWS - 1:CARRY_ROWS, :]
    prev2 = carry[CARRY_ROWS - 2:CARRY_ROWS - 1, :]
    tm, tn = g_ref.shape
    row = lax.broadcasted_iota(jnp.int32, (tm, tn), 0)
    u, g = _gated_conv(
        x_ref[...], wb_ref, wc_ref, wh_ref, wz_ref, cw_ref,
        lambda r1: jnp.where(row == 0, prev1, r1),
        lambda r2: jnp.where(row == 0, prev2, jnp.where(row == 1, prev1, r2)))
    tail = u[tm - CARRY_ROWS:, :]
    carry_ref[...] = tail
    tail_ref[0] = tail
    g_ref[...] = g.astype(g_ref.dtype)


def _conv_small_kernel(x_ref, wb_ref, wc_ref, wh_ref, wz_ref, cw_ref,
                       m1_ref, m2_ref, p1_ref, p2_ref, g_ref, u_ref):
    u, g = _gated_conv(
        x_ref[...], wb_ref, wc_ref, wh_ref, wz_ref, cw_ref,
        lambda r1: r1 * m1_ref[...] + p1_ref[...],
        lambda r2: r2 * m2_ref[...] + p2_ref[...])
    u_ref[...] = u
    g_ref[...] = g.astype(g_ref.dtype)


def _w_in_specs(d, dc, tn):
    nj = dc // tn
    return [pl.BlockSpec((d, tn), functools.partial(lambda c, j, i: (0, c * nj + j), c))
            for c in range(4)]


def conv_mixer_big(xn, w_in, conv_w, ctx, seq_len, tm=1024, tn=256):
    rows, d = xn.shape
    dc = w_in.shape[1] // 4
    tm = min(tm, seq_len)
    tps = seq_len // tm
    n_seq = rows // seq_len
    return pl.pallas_call(
        functools.partial(_conv_big_kernel, tiles_per_seq=tps),
        grid=(dc // tn, rows // tm),
        in_specs=[pl.BlockSpec((tm, d), lambda j, i: (i, 0))] + _w_in_specs(d, dc, tn) + [
            pl.BlockSpec((CONV_WIDTH, tn), lambda j, i: (0, j)),
            pl.BlockSpec((CARRY_ROWS, tn), lambda j, i: (0, j))],
        out_specs=[pl.BlockSpec((tm, tn), lambda j, i: (i, j)),
                   pl.BlockSpec((1, CARRY_ROWS, tn), lambda j, i: (i // tps, 0, j))],
        out_shape=[jax.ShapeDtypeStruct((rows, dc), BF16),
                   jax.ShapeDtypeStruct((n_seq, CARRY_ROWS, dc), F32)],
        scratch_shapes=[pltpu.VMEM((CARRY_ROWS, tn), F32)],
        compiler_params=_params("arbitrary", "arbitrary"),
        name="conv_mixer_big",
    )(xn, w_in, w_in, w_in, w_in, conv_w, ctx)


def conv_mixer_small(xn, w_in, conv_w, m1, m2, p1, p2, tn=256):
    rows, d = xn.shape
    dc = w_in.shape[1] // 4
    return pl.pallas_call(
        _conv_small_kernel,
        grid=(dc // tn, 1),
        in_specs=[pl.BlockSpec((rows, d), lambda j, i: (0, 0))] + _w_in_specs(d, dc, tn) + [
            pl.BlockSpec((CONV_WIDTH, tn), lambda j, i: (0, j)),
            pl.BlockSpec((rows, 1), lambda j, i: (0, 0)),
            pl.BlockSpec((rows, 1), lambda j, i: (0, 0)),
            pl.BlockSpec((rows, tn), lambda j, i: (0, j)),
            pl.BlockSpec((rows, tn), lambda j, i: (0, j))],
        out_specs=[pl.BlockSpec((rows, tn), lambda j, i: (0, j)),
                   pl.BlockSpec((rows, tn), lambda j, i: (0, j))],
        out_shape=[jax.ShapeDtypeStruct((rows, dc), BF16),
                   jax.ShapeDtypeStruct((rows, dc), F32)],
        compiler_params=_params("parallel", "arbitrary"),
        name="conv_mixer_small",
    )(xn, w_in, w_in, w_in, w_in, conv_w, m1, m2, p1, p2)


def _proj_residual_kernel(x_ref, w_ref, h_ref, g_ref, o_ref):
    a = _dot(x_ref[...], w_ref[...])
    inv = lax.rsqrt(jnp.mean(a * a, axis=-1, keepdims=True) + RMS_EPS)
    o_ref[...] = h_ref[...] + a * inv * g_ref[...]


def proj_residual(x, w, h, gain, tm=256):
    rows, kdim = x.shape
    d = w.shape[1]
    tm = _row_tile(rows, tm)
    return pl.pallas_call(
        _proj_residual_kernel,
        grid=(rows // tm,),
        in_specs=[pl.BlockSpec((tm, kdim), lambda i: (i, 0)),
                  pl.BlockSpec((kdim, d), lambda i: (0, 0), pipeline_mode=pl.Buffered(1)),
                  pl.BlockSpec((tm, d), lambda i: (i, 0)),
                  pl.BlockSpec((1, d), lambda i: (0, 0))],
        out_specs=pl.BlockSpec((tm, d), lambda i: (i, 0)),
        out_shape=jax.ShapeDtypeStruct((rows, d), F32),
        compiler_params=_params("arbitrary"),
        name="proj_residual",
    )(x, w, h, gain)


def _mm_kernel(x_ref, xs_ref, w_ref, o_ref, os_ref, wb_ref, *, epilogue, dot):
    @pl.when(pl.program_id(1) == 0)
    def _():
        wb_ref[...] = w_ref[...].astype(BF16)
        os_ref[...] = epilogue(dot(xs_ref[...], wb_ref[...])).astype(os_ref.dtype)

    o_ref[...] = epilogue(dot(x_ref[...], wb_ref[...])).astype(o_ref.dtype)


def matmul(x, xs, w, col0, n, out_dtype, epilogue=lambda y: y, w_is_transposed=False,
           tm=1024, tn=1024):
    rows, kdim = x.shape
    rows_s = xs.shape[0]
    tm = _row_tile(rows, tm)
    tn = min(tn, n)
    j0 = col0 // tn
    if w_is_transposed:
        w_block, w_index, dot = (tn, kdim), (lambda j, i: (j0 + j, 0)), _dot_nt
    else:
        w_block, w_index, dot = (kdim, tn), (lambda j, i: (0, j0 + j)), _dot
    w_spec = pl.BlockSpec(w_block, w_index, pipeline_mode=pl.Buffered(1))
    return pl.pallas_call(
        functools.partial(_mm_kernel, epilogue=epilogue, dot=dot),
        grid=(n // tn, rows // tm),
        in_specs=[pl.BlockSpec((tm, kdim), lambda j, i: (i, 0)),
                  pl.BlockSpec((rows_s, kdim), lambda j, i: (0, 0)),
                  w_spec],
        out_specs=[pl.BlockSpec((tm, tn), lambda j, i: (i, j)),
                   pl.BlockSpec((rows_s, tn), lambda j, i: (0, j))],
        out_shape=[jax.ShapeDtypeStruct((rows, n), out_dtype),
                   jax.ShapeDtypeStruct((rows_s, n), out_dtype)],
        scratch_shapes=[pltpu.VMEM(w_block, BF16)],
        compiler_params=_params("parallel", "arbitrary"),
        name="matmul",
    )(x, xs, w)


def _silu(z):
    return z * _sigmoid(z)


def _forget_gate_kernel(x_ref, w_ref, b_ref, o_ref):
    t = _dot(x_ref[...], w_ref[...]) + b_ref[...]
    o_ref[...] = jnp.minimum(t, 0.0) - jnp.log(1.0 + jnp.exp(-jnp.abs(t)))


def forget_gate(x, w_f, b_f):
    rows, kdim = x.shape
    tm = _row_tile(rows, 1024)
    return pl.pallas_call(
        _forget_gate_kernel,
        grid=(rows // tm,),
        in_specs=[pl.BlockSpec((tm, kdim), lambda i: (i, 0)),
                  pl.BlockSpec((kdim, LANES), lambda i: (0, 0)),
                  pl.BlockSpec((1, LANES), lambda i: (0, 0))],
        out_specs=pl.BlockSpec((tm, LANES), lambda i: (i, 0)),
        out_shape=jax.ShapeDtypeStruct((rows, LANES), F32),
        compiler_params=_params("parallel"),
        name="forget_gate",
    )(x, w_f, b_f)


def _cumsum_kernel(x_ref, o_ref, *, n_chunks):
    r = lax.broadcasted_iota(jnp.int32, (LANES, LANES), 0)
    c = lax.broadcasted_iota(jnp.int32, (LANES, LANES), 1)
    tri = jnp.where(c <= r, 1.0, 0.0).astype(BF16)
    carry = jnp.zeros((1, LANES), F32)
    for n in range(n_chunks):
        hi, mid, lo = _split3(x_ref[0, n * LANES:(n + 1) * LANES, :])
        f = _dot(tri, hi) + _dot(tri, mid) + _dot(tri, lo) + carry
        o_ref[0, n * LANES:(n + 1) * LANES, :] = f
        carry = f[LANES - 1:LANES, :]


def cumsum_rows(x):
    b, length, lanes = x.shape
    return pl.pallas_call(
        functools.partial(_cumsum_kernel, n_chunks=length // LANES),
        grid=(b,),
        in_specs=[pl.BlockSpec((1, length, lanes), lambda i: (i, 0, 0))],
        out_specs=pl.BlockSpec((1, length, lanes), lambda i: (i, 0, 0)),
        out_shape=jax.ShapeDtypeStruct(x.shape, F32),
        compiler_params=_params("parallel"),
        name="cumsum_rows",
    )(x)


def _attn_prompt_kernel(q_ref, k_ref, v_ref, f_ref, fm_ref, frow_ref,
                        sz_ref, o_ref, kb_ref, vt_ref, fcol_ref, s_ref, *, heads_per_step,
                        n_meta, tq):
    hg = pl.program_id(1)
    lq = q_ref.shape[0]
    nq = lq // tq
    n_heads = f_ref.shape[2]
    pad = LANES - n_meta
    key = lax.broadcasted_iota(jnp.int32, (tq, tq), 0)
    qry = lax.broadcasted_iota(jnp.int32, (tq, tq), 1)
    pad_row = lax.broadcasted_iota(jnp.int32, (LANES, 1), 0) < pad

    for hh in range(heads_per_step):
        sl = slice(hh * HEAD_DIM, (hh + 1) * HEAD_DIM)
        head = hg * heads_per_step + hh
        zeros = jnp.zeros((pad, HEAD_DIM), F32)
        k_meta, v_meta = k_ref[0:n_meta, sl], v_ref[0:n_meta, sl]
        kb_ref[hh, 0:LANES, :] = jnp.concatenate([zeros, k_meta], axis=0).astype(BF16)
        kb_ref[hh, LANES:, :] = k_ref[n_meta:, sl].astype(BF16)
        vt_ref[hh, :, 0:LANES] = jnp.concatenate([zeros, v_meta], axis=0).T.astype(BF16)
        for c in range(nq):
            vt_ref[hh, :, LANES + c * tq:LANES + (c + 1) * tq] = (
                v_ref[n_meta + c * tq:n_meta + (c + 1) * tq, sl].T.astype(BF16))
        pick_m = lax.broadcasted_iota(jnp.int32, (LANES, n_heads), 1) == head
        f_meta = jnp.sum(jnp.where(pick_m, fm_ref[0], 0.0), axis=-1, keepdims=True)
        fcol_ref[hh, 0:LANES, :] = jnp.where(pad_row, -MASKED, f_meta)
        pick = lax.broadcasted_iota(jnp.int32, (lq, n_heads), 1) == head
        fcol_ref[hh, LANES:, :] = jnp.sum(jnp.where(pick, f_ref[0], 0.0), axis=-1, keepdims=True)

    for qi in range(nq):
        rows = slice(qi * tq, (qi + 1) * tq)
        tiles = [(0, LANES)] + [(LANES + j * tq, LANES + (j + 1) * tq) for j in range(qi + 1)]
        for hh in range(heads_per_step):
            sl = slice(hh * HEAD_DIM, (hh + 1) * HEAD_DIM)
            buf = (qi * heads_per_step + hh) % s_ref.shape[0]
            q = q_ref[rows, sl]
            f_row = frow_ref[0, hh, qi]
            m = None
            for a, b in tiles:
                s = _dot_nt(kb_ref[hh, a:b, :], q) + (f_row - fcol_ref[hh, a:b, :])
                if b == tiles[-1][1]:
                    s = jnp.where(key <= qry, s, MASKED)
                s_ref[buf, a:b, :] = s
                tile_max = jnp.max(s, axis=0, keepdims=True)
                m = tile_max if m is None else jnp.maximum(m, tile_max)
            l = jnp.zeros((1, tq), F32)
            acc = jnp.zeros((HEAD_DIM, tq), F32)
            for a, b in tiles:
                p = jnp.exp(s_ref[buf, a:b, :] - m)
                l = l + jnp.sum(p, axis=0, keepdims=True)
                acc = acc + _dot(vt_ref[hh, :, a:b], p.astype(BF16))
            gate = sz_ref[rows, sl].astype(F32)
            o_ref[rows, sl] = ((acc / l).T * gate).astype(o_ref.dtype)


def attention_prompt(q, k, v, f_nat, f_meta, f_rows, sz, *, n_prompts, n_meta,
                     heads_per_step=1, tq=256):
    rows, d = q.shape
    lq = rows // n_prompts
    n_heads = d // HEAD_DIM
    w = heads_per_step * HEAD_DIM
    big = pl.BlockSpec((lq, w), lambda b, h: (b, h))
    keys = pl.BlockSpec((n_meta + lq, w), lambda b, h: (b, h))
    return pl.pallas_call(
        functools.partial(_attn_prompt_kernel, heads_per_step=heads_per_step,
                          n_meta=n_meta, tq=tq),
        grid=(n_prompts, n_heads // heads_per_step),
        in_specs=[big, keys, keys,
                  pl.BlockSpec((1, lq, n_heads), lambda b, h: (b, 0, 0)),
                  pl.BlockSpec((1, LANES, n_heads), lambda b, h: (b, 0, 0)),
                  pl.BlockSpec((1, heads_per_step, lq // tq, 1, tq),
                               lambda b, h: (b, h, 0, 0, 0)),
                  big],
        out_specs=big,
        out_shape=jax.ShapeDtypeStruct((rows, d), BF16),
        scratch_shapes=[pltpu.VMEM((heads_per_step, LANES + lq, HEAD_DIM), BF16),
                        pltpu.VMEM((heads_per_step, HEAD_DIM, LANES + lq), BF16),
                        pltpu.VMEM((heads_per_step, LANES + lq, 1), F32),
                        pltpu.VMEM((2, LANES + lq, tq), F32)],
        compiler_params=_params("parallel", "parallel"),
        name="attention_prompt",
    )(q, k, v, f_nat, f_meta, f_rows, sz)


def _attn_decode_kernel(pt_ref, *refs, pages, n_chunks, n_new, n_heads):
    k_refs = refs[0:pages]
    v_refs = refs[pages:2 * pages]
    lf_refs = refs[2 * pages:3 * pages]
    (qbd_ref, kn_ref, vn_ref, lfn_ref, sz_ref, o_ref,
     kc_ref, vc_ref, acc_ref, m_ref, l_ref, crow_ref, tail_ref, s_ref) = refs[3 * pages:]
    c = pl.program_id(1)
    page = lf_refs[0].shape[1]
    ncol = n_heads * n_new
    qbd = qbd_ref[0]

    def to_col(x):
        return jnp.broadcast_to(x, (LANES, ncol)).T[:, 0:1]

    @pl.when(c == 0)
    def _():
        rows_new = lfn_ref.shape[1]
        r = lax.broadcasted_iota(jnp.int32, (rows_new, ncol), 0)
        qcol = lax.broadcasted_iota(jnp.int32, (rows_new, ncol), 1) & (n_new - 1)
        lfn = lfn_ref[0]
        cum = jnp.zeros((rows_new, ncol), F32)
        for j in range(n_new):
            cum = cum + jnp.where(r >= j, lfn[j:j + 1, :], 0.0)
        crow = jnp.sum(jnp.where(r == qcol, cum, 0.0), axis=0, keepdims=True)
        s = _dot(kn_ref[0], qbd) + (crow - cum)
        s = jnp.where(r <= qcol, s, MASKED)
        m = jnp.max(s, axis=0, keepdims=True)
        p = jnp.exp(s - m)
        m_ref[...] = m
        l_ref[...] = jnp.sum(p, axis=0, keepdims=True)
        acc_ref[...] = _dot(p.T.astype(BF16), vn_ref[0])
        crow_ref[...] = crow
        tail_ref[...] = jnp.zeros_like(tail_ref)
        s_ref[...] = jnp.full(s_ref.shape, MASKED, F32)

    def head_major(page_ref, dst_ref, p_i):
        heads = jnp.swapaxes(page_ref[0], 0, 1).astype(BF16)
        for h in range(n_heads):
            dst_ref[p_i * page:(p_i + 1) * page, h * HEAD_DIM:(h + 1) * HEAD_DIM] = heads[h]

    s_prev = s_ref[...]
    m_old = m_ref[...]
    m_new = jnp.maximum(m_old, jnp.max(s_prev, axis=0, keepdims=True))
    alpha = jnp.exp(m_old - m_new)
    p = jnp.exp(s_prev - m_new)
    m_ref[...] = m_new
    l_ref[...] = alpha * l_ref[...] + jnp.sum(p, axis=0, keepdims=True)
    for p_i in range(pages):
        head_major(v_refs[p_i], vc_ref, p_i)
    acc_ref[...] = acc_ref[...] * to_col(alpha) + _dot(p.T.astype(BF16), vc_ref[...])

    ri = lax.broadcasted_iota(jnp.int32, (page, page), 0)
    ci = lax.broadcasted_iota(jnp.int32, (page, page), 1)
    upper = jnp.where(ci > ri, 1.0, 0.0).astype(BF16)
    er = lax.broadcasted_iota(jnp.int32, (LANES, ncol), 0)
    ec = lax.broadcasted_iota(jnp.int32, (LANES, ncol), 1)
    expand = jnp.where(er * n_new == (ec & ~(n_new - 1)), 1.0, 0.0).astype(BF16)
    tail = tail_ref[...]
    decay = [None] * pages
    for p_i in reversed(range(pages)):
        parts = [_dot(x, expand).astype(BF16) for x in _split3(lf_refs[p_i][0])]
        g = tail
        for e in parts:
            g = g + _dot(upper, e)
        decay[p_i] = g
        for e in parts:
            tail = tail + jnp.sum(e.astype(F32), axis=0, keepdims=True)
        head_major(k_refs[p_i], kc_ref, p_i)
    tail_ref[...] = tail
    s_ref[...] = _dot(kc_ref[...], qbd) + (crow_ref[...] + jnp.concatenate(decay, axis=0))

    @pl.when(c == n_chunks)
    def _():
        inv_l = to_col(1.0 / l_ref[...])
        outs = []
        for h in range(n_heads):
            blk = acc_ref[h * n_new:(h + 1) * n_new, h * HEAD_DIM:(h + 1) * HEAD_DIM]
            outs.append(blk * inv_l[h * n_new:(h + 1) * n_new, :])
        o = jnp.concatenate(outs, axis=1)
        o_ref[0] = (o * sz_ref[0].astype(F32)).astype(o_ref.dtype)


def attention_decode(page_table, cache_k, cache_v, cache_lf, qbd, k_new, v_new, lf_new, sz,
                     *, n_new, pages=4):
    nb, n_pages = page_table.shape
    _, page, n_heads, _ = cache_k.shape
    d = n_heads * HEAD_DIM
    ncol = n_heads * n_new
    n_chunks = n_pages // pages

    def paged(lag, *tail):
        zeros = (0,) * len(tail)

        def index(p, b, c, pt):
            chunk = jnp.clip(c - lag, 0, n_chunks - 1)
            return (pt[b, (n_chunks - 1 - chunk) * pages + p],) + zeros

        return [pl.BlockSpec((1,) + tail, functools.partial(index, p)) for p in range(pages)]

    per_seq = lambda shape: pl.BlockSpec((1,) + shape, lambda b, c, pt: (b, 0, 0))
    grid_spec = pltpu.PrefetchScalarGridSpec(
        num_scalar_prefetch=1,
        grid=(nb, n_chunks + 1),
        in_specs=paged(0, page, n_heads, HEAD_DIM) + paged(1, page, n_heads, HEAD_DIM)
        + paged(0, page, LANES) + [
            per_seq((d, ncol)), per_seq((LANES, d)), per_seq((LANES, d)),
            per_seq((LANES, ncol)), per_seq((n_new, d))],
        out_specs=per_seq((n_new, d)),
        scratch_shapes=[pltpu.VMEM((pages * page, d), BF16), pltpu.VMEM((pages * page, d), BF16),
                        pltpu.VMEM((ncol, d), F32), pltpu.VMEM((1, ncol), F32),
                        pltpu.VMEM((1, ncol), F32), pltpu.VMEM((1, ncol), F32),
                        pltpu.VMEM((1, ncol), F32), pltpu.VMEM((pages * page, ncol), F32)],
    )
    return pl.pallas_call(
        functools.partial(_attn_decode_kernel, pages=pages, n_chunks=n_chunks, n_new=n_new,
                          n_heads=n_heads),
        grid_spec=grid_spec,
        out_shape=jax.ShapeDtypeStruct((nb, n_new, d), BF16),
        compiler_params=_params("parallel", "arbitrary"),
        name="attention_decode",
    )(page_table, *([cache_k] * pages), *([cache_v] * pages), *([cache_lf] * pages),
      qbd, k_new, v_new, lf_new, sz)


def kernel(x_prompt, x_sample, state_conv, cache_k, cache_v, cache_logf, page_table,
           meta_tokens, a_pre_g, a_post_g, a_w_in, a_conv_w, a_w_out, kv_g, w_kvf, b_f,
           b_pre_g, b_post_g, b_w_qz, b_w_o):
    n_prompts, seq, d = x_prompt.shape
    n_dec, n_new, _ = x_sample.shape
    n_meta = meta_tokens.shape[0]
    d_attn = b_w_o.shape[1]
    n_heads = d_attn // HEAD_DIM
    dc = a_conv_w.shape[2]
    n_small = n_meta + n_dec * n_new
    scale = HEAD_DIM ** -0.5

    w_in = a_w_in[0].astype(BF16)
    w_out = a_w_out[0].astype(BF16)
    w_f = jnp.pad(w_kvf[:, 2 * d_attn:], ((0, 0), (0, LANES - n_heads))).astype(BF16)
    bias_f = jnp.pad(b_f, (0, LANES - n_heads)).reshape(1, LANES)
    w_qz = b_w_qz[0]
    w_o = b_w_o[0].astype(BF16)

    x_big = x_prompt.reshape(n_prompts * seq, d)
    x_small = jnp.concatenate([meta_tokens, x_sample.reshape(n_dec * n_new, d)], axis=0)

    starts = jnp.arange(n_small)
    in_dec = starts >= n_meta
    pos = jnp.where(in_dec, (starts - n_meta) % n_new, starts)
    m1 = (pos >= 1).astype(F32).reshape(n_small, 1)
    m2 = (pos >= 2).astype(F32).reshape(n_small, 1)
    st = state_conv[0]
    p1 = jnp.zeros((n_dec, n_new, dc), F32).at[:, 0].set(st[:, 1])
    p2 = jnp.zeros((n_dec, n_new, dc), F32).at[:, 0].set(st[:, 0]).at[:, 1].set(st[:, 1])
    zeros_meta = jnp.zeros((n_meta, dc), F32)
    p1 = jnp.concatenate([zeros_meta, p1.reshape(n_dec * n_new, dc)], axis=0)
    p2 = jnp.concatenate([zeros_meta, p2.reshape(n_dec * n_new, dc)], axis=0)

    (xn_small,) = rms_norm_bf16(x_small, a_pre_g)
    g_small, u_small = conv_mixer_small(xn_small, w_in, a_conv_w[0], m1, m2, p1, p2)
    (xn_big,) = rms_norm_bf16(x_big, a_pre_g)
    ctx = jnp.zeros((CARRY_ROWS, dc), F32).at[CARRY_ROWS - 2:].set(u_small[n_meta - 2:n_meta])
    g_big, tails = conv_mixer_big(xn_big, w_in, a_conv_w[0], ctx, seq)
    h_small = proj_residual(g_small, w_out, x_small, a_post_g)
    h_big = proj_residual(g_big, w_out, x_big, a_post_g)

    state_conv_prompt = tails[None, :, CARRY_ROWS - 2:, :]
    u_dec = u_small[n_meta:].reshape(n_dec, n_new, dc)
    state_conv_sample = u_dec[None, :, n_new - 2:, :]

    gains = jnp.stack([kv_g, b_pre_g[0]], axis=0)
    hkv_small, hb_small = rms_norm_bf16(h_small, gains)
    hkv_big, hb_big = rms_norm_bf16(h_big, gains)

    total = n_meta + seq
    hkv_full = jnp.concatenate(
        [jnp.broadcast_to(hkv_small[None, :n_meta], (n_prompts, n_meta, d)),
         hkv_big.reshape(n_prompts, seq, d)], axis=1).reshape(n_prompts * total, d)
    w_kvf_t = w_kvf.T
    k_full, k_small = matmul(hkv_full, hkv_small, w_kvf_t, 0, d_attn, F32,
                             w_is_transposed=True)
    v_full, v_small = matmul(hkv_full, hkv_small, w_kvf_t, d_attn, d_attn, F32,
                             w_is_transposed=True)
    lf_full = forget_gate(hkv_full, w_f, bias_f)
    lf_small = forget_gate(hkv_small, w_f, bias_f)
    q_big, q_small = matmul(hb_big, hb_small, w_qz, 0, d_attn, BF16,
                            epilogue=lambda y: y * scale)
    sz_big, sz_small = matmul(hb_big, hb_small, w_qz, d_attn, d_attn, BF16, epilogue=_silu)

    lf_prompt = lf_full.reshape(n_prompts, total, LANES)
    padded = -(-total // LANES) * LANES
    f_prompt = cumsum_rows(jnp.pad(lf_prompt, ((0, 0), (0, padded - total), (0, 0))))

    tq = 256
    f_nat = f_prompt[:, n_meta:total, :n_heads]
    f_rows = jnp.transpose(f_nat, (0, 2, 1)).reshape(n_prompts, n_heads, seq // tq, 1, tq)
    f_meta = jnp.pad(f_prompt[:, :n_meta, :n_heads], ((0, 0), (LANES - n_meta, 0), (0, 0)))
    og_big = attention_prompt(q_big, k_full, v_full, f_nat, f_meta, f_rows, sz_big,
                              n_prompts=n_prompts, n_meta=n_meta, tq=tq)
    y_prompt = proj_residual(og_big, w_o, h_big, b_post_g).reshape(n_prompts, seq, d)

    n_rows_dec = n_dec * n_new
    q_dec = q_small[n_meta:].reshape(n_dec, n_new, n_heads, HEAD_DIM)
    q_t = jnp.transpose(q_dec, (0, 2, 3, 1)).reshape(n_dec, d_attn, n_new)
    same_head = (jnp.arange(d_attn)[:, None] // HEAD_DIM) == (jnp.arange(n_heads * n_new)[None, :] // n_new)
    qbd = jnp.where(same_head[None], jnp.tile(q_t, (1, 1, n_heads)), jnp.zeros((), BF16))
    pad_rows = ((0, 0), (0, LANES - n_new), (0, 0))
    k_new = jnp.pad(k_small[n_meta:].reshape(n_dec, n_new, d_attn).astype(BF16), pad_rows)
    v_new = jnp.pad(v_small[n_meta:].reshape(n_dec, n_new, d_attn).astype(BF16), pad_rows)
    lf_dec = lf_small[n_meta:, :n_heads].reshape(n_dec, n_new, n_heads)
    lf_new = jnp.pad(jnp.repeat(lf_dec, n_new, axis=2), pad_rows)
    pool, page = cache_k.shape[0], cache_k.shape[1]
    cache_lf = jnp.pad(cache_logf, ((0, 0), (0, 0), (0, LANES - n_heads)))
    og_dec = attention_decode(
        page_table, cache_k, cache_v, cache_lf, qbd, k_new, v_new, lf_new, sz_small[n_meta:].reshape(n_dec, n_new, d_attn),
        n_new=n_new)
    y_sample = proj_residual(og_dec.reshape(n_rows_dec, d_attn), w_o, h_small[n_meta:],
                             b_post_g).reshape(n_dec, n_new, d)

    k_prompt = k_full.reshape(n_prompts, total, n_heads, HEAD_DIM)
    v_prompt = v_full.reshape(n_prompts, total, n_heads, HEAD_DIM)
    logf_prompt = lf_prompt[:, :, :n_heads]
    k_sample = k_small[n_meta:].reshape(n_dec, n_new, n_heads, HEAD_DIM)
    v_sample = v_small[n_meta:].reshape(n_dec, n_new, n_heads, HEAD_DIM)
    logf_sample = lf_dec
    return (y_prompt, y_sample, state_conv_prompt, state_conv_sample,
            k_prompt, v_prompt, logf_prompt, k_sample, v_sample, logf_sample)
```

```python
import functools

import jax
import jax.numpy as jnp
from jax import lax
from jax.experimental import pallas as pl
from jax.experimental.pallas import tpu as pltpu

F32 = jnp.float32
BF16 = jnp.bfloat16

RMS_EPS = 1e-6
CONV_WIDTH = 3
HEAD_DIM = 128
LANES = 128
CARRY_ROWS = 8
MASKED = -1e30
VMEM_LIMIT_BYTES = 60 * 1024 * 1024


def _params(*sem):
    return pltpu.CompilerParams(dimension_semantics=sem, vmem_limit_bytes=VMEM_LIMIT_BYTES)


def _dot(a, b):
    return jnp.dot(a, b, preferred_element_type=F32)


def _dot_nt(a, b):
    return lax.dot_general(a, b, (((1,), (1,)), ((), ())), preferred_element_type=F32)


def _split3(x):
    hi = x.astype(BF16)
    r1 = x - hi.astype(F32)
    mid = r1.astype(BF16)
    lo = (r1 - mid.astype(F32)).astype(BF16)
    return hi, mid, lo


def _sigmoid(z):
    return 1.0 / (1.0 + jnp.exp(-z))


BF16_ROWS = 16


def _row_tile(rows, want):
    for t in range(min(want, rows) // BF16_ROWS * BF16_ROWS, 0, -BF16_ROWS):
        if rows % t == 0:
            return t
    return rows


def _norm_kernel(x_ref, g_ref, *o_refs):
    x = x_ref[...]
    xhat = x * lax.rsqrt(jnp.mean(x * x, axis=-1, keepdims=True) + RMS_EPS)
    for n, o_ref in enumerate(o_refs):
        o_ref[...] = (xhat * g_ref[n:n + 1, :]).astype(o_ref.dtype)


def rms_norm_bf16(x, gains):
    rows, d = x.shape
    n = gains.shape[0]
    tm = _row_tile(rows, 512)
    return pl.pallas_call(
        _norm_kernel,
        grid=(rows // tm,),
        in_specs=[pl.BlockSpec((tm, d), lambda i: (i, 0)),
                  pl.BlockSpec((n, d), lambda i: (0, 0))],
        out_specs=[pl.BlockSpec((tm, d), lambda i: (i, 0))] * n,
        out_shape=[jax.ShapeDtypeStruct((rows, d), BF16)] * n,
        compiler_params=_params("parallel"),
        name="rms_norm",
    )(x, gains)


def _gated_conv(x, wb_ref, wc_ref, wh_ref, wz_ref, cw_ref, um1_fix, um2_fix):
    u = _dot(x, wc_ref[...]) * _dot(x, wh_ref[...])
    um1 = um1_fix(pltpu.roll(u, 1, 0))
    um2 = um2_fix(pltpu.roll(u, 2, 0))
    cw = cw_ref[...]
    y = cw[0:1, :] * um2 + cw[1:2, :] * um1 + cw[2:3, :] * u
    z = _dot(x, wz_ref[...])
    g = _dot(x, wb_ref[...]) * y * (z * _sigmoid(z))
    return u, g


def _conv_big_kernel(x_ref, wb_ref, wc_ref, wh_ref, wz_ref, cw_ref, ctx_ref,
                     g_ref, tail_ref, carry_ref, *, tiles_per_seq):
    i = pl.program_id(1)

    @pl.when(i % tiles_per_seq == 0)
    def _():
        carry_ref[...] = ctx_ref[...]

    carry = carry_ref[...]
    prev1 = carry[CARRY_ROWS - 1:CARRY_ROWS, :]
    prev2 = carry[CARRY_ROWS - 2:CARRY_ROWS - 1, :]
    tm, tn = g_ref.shape
    row = lax.broadcasted_iota(jnp.int32, (tm, tn), 0)
    u, g = _gated_conv(
        x_ref[...], wb_ref, wc_ref, wh_ref, wz_ref, cw_ref,
        lambda r1: jnp.where(row == 0, prev1, r1),
        lambda r2: jnp.where(row == 0, prev2, jnp.where(row == 1, prev1, r2)))
    tail = u[tm - CARRY_ROWS:, :]
    carry_ref[...] = tail
    tail_ref[0] = tail
    g_ref[...] = g.astype(g_ref.dtype)


def _conv_small_kernel(x_ref, wb_ref, wc_ref, wh_ref, wz_ref, cw_ref,
                       m1_ref, m2_ref, p1_ref, p2_ref, g_ref, u_ref):
    u, g = _gated_conv(
        x_ref[...], wb_ref, wc_ref, wh_ref, wz_ref, cw_ref,
        lambda r1: r1 * m1_ref[...] + p1_ref[...],
        lambda r2: r2 * m2_ref[...] + p2_ref[...])
    u_ref[...] = u
    g_ref[...] = g.astype(g_ref.dtype)


def _w_in_specs(d, dc, tn):
    nj = dc // tn
    return [pl.BlockSpec((d, tn), functools.partial(lambda c, j, i: (0, c * nj + j), c))
            for c in range(4)]


def conv_mixer_big(xn, w_in, conv_w, ctx, seq_len, tm=1024, tn=256):
    rows, d = xn.shape
    dc = w_in.shape[1] // 4
    tm = min(tm, seq_len)
    tps = seq_len // tm
    n_seq = rows // seq_len
    return pl.pallas_call(
        functools.partial(_conv_big_kernel, tiles_per_seq=tps),
        grid=(dc // tn, rows // tm),
        in_specs=[pl.BlockSpec((tm, d), lambda j, i: (i, 0))] + _w_in_specs(d, dc, tn) + [
            pl.BlockSpec((CONV_WIDTH, tn), lambda j, i: (0, j)),
            pl.BlockSpec((CARRY_ROWS, tn), lambda j, i: (0, j))],
        out_specs=[pl.BlockSpec((tm, tn), lambda j, i: (i, j)),
                   pl.BlockSpec((1, CARRY_ROWS, tn), lambda j, i: (i // tps, 0, j))],
        out_shape=[jax.ShapeDtypeStruct((rows, dc), BF16),
                   jax.ShapeDtypeStruct((n_seq, CARRY_ROWS, dc), F32)],
        scratch_shapes=[pltpu.VMEM((CARRY_ROWS, tn), F32)],
        compiler_params=_params("arbitrary", "arbitrary"),
        name="conv_mixer_big",
    )(xn, w_in, w_in, w_in, w_in, conv_w, ctx)


def conv_mixer_small(xn, w_in, conv_w, m1, m2, p1, p2, tn=256):
    rows, d = xn.shape
    dc = w_in.shape[1] // 4
    return pl.pallas_call(
        _conv_small_kernel,
        grid=(dc // tn, 1),
        in_specs=[pl.BlockSpec((rows, d), lambda j, i: (0, 0))] + _w_in_specs(d, dc, tn) + [
            pl.BlockSpec((CONV_WIDTH, tn), lambda j, i: (0, j)),
            pl.BlockSpec((rows, 1), lambda j, i: (0, 0)),
            pl.BlockSpec((rows, 1), lambda j, i: (0, 0)),
            pl.BlockSpec((rows, tn), lambda j, i: (0, j)),
            pl.BlockSpec((rows, tn), lambda j, i: (0, j))],
        out_specs=[pl.BlockSpec((rows, tn), lambda j, i: (0, j)),
                   pl.BlockSpec((rows, tn), lambda j, i: (0, j))],
        out_shape=[jax.ShapeDtypeStruct((rows, dc), BF16),
                   jax.ShapeDtypeStruct((rows, dc), F32)],
        compiler_params=_params("parallel", "arbitrary"),
        name="conv_mixer_small",
    )(xn, w_in, w_in, w_in, w_in, conv_w, m1, m2, p1, p2)


def _proj_residual_kernel(x_ref, w_ref, h_ref, g_ref, o_ref):
    a = _dot(x_ref[...], w_ref[...])
    inv = lax.rsqrt(jnp.mean(a * a, axis=-1, keepdims=True) + RMS_EPS)
    o_ref[...] = h_ref[...] + a * inv * g_ref[...]


def proj_residual(x, w, h, gain, tm=256):
    rows, kdim = x.shape
    d = w.shape[1]
    tm = _row_tile(rows, tm)
    return pl.pallas_call(
        _proj_residual_kernel,
        grid=(rows // tm,),
        in_specs=[pl.BlockSpec((tm, kdim), lambda i: (i, 0)),
                  pl.BlockSpec((kdim, d), lambda i: (0, 0), pipeline_mode=pl.Buffered(1)),
                  pl.BlockSpec((tm, d), lambda i: (i, 0)),
                  pl.BlockSpec((1, d), lambda i: (0, 0))],
        out_specs=pl.BlockSpec((tm, d), lambda i: (i, 0)),
        out_shape=jax.ShapeDtypeStruct((rows, d), F32),
        compiler_params=_params("arbitrary"),
        name="proj_residual",
    )(x, w, h, gain)


def _mm_kernel(x_ref, xs_ref, w_ref, o_ref, os_ref, wb_ref, *, epilogue, dot):
    @pl.when(pl.program_id(1) == 0)
    def _():
        wb_ref[...] = w_ref[...].astype(BF16)
        os_ref[...] = epilogue(dot(xs_ref[...], wb_ref[...])).astype(os_ref.dtype)

    o_ref[...] = epilogue(dot(x_ref[...], wb_ref[...])).astype(o_ref.dtype)


def matmul(x, xs, w, col0, n, out_dtype, epilogue=lambda y: y, w_is_transposed=False,
           tm=1024, tn=1024):
    rows, kdim = x.shape
    rows_s = xs.shape[0]
    tm = _row_tile(rows, tm)
    tn = min(tn, n)
    j0 = col0 // tn
    if w_is_transposed:
        w_block, w_index, dot = (tn, kdim), (lambda j, i: (j0 + j, 0)), _dot_nt
    else:
        w_block, w_index, dot = (kdim, tn), (lambda j, i: (0, j0 + j)), _dot
    w_spec = pl.BlockSpec(w_block, w_index, pipeline_mode=pl.Buffered(1))
    return pl.pallas_call(
        functools.partial(_mm_kernel, epilogue=epilogue, dot=dot),
        grid=(n // tn, rows // tm),
        in_specs=[pl.BlockSpec((tm, kdim), lambda j, i: (i, 0)),
                  pl.BlockSpec((rows_s, kdim), lambda j, i: (0, 0)),
                  w_spec],
        out_specs=[pl.BlockSpec((tm, tn), lambda j, i: (i, j)),
                   pl.BlockSpec((rows_s, tn), lambda j, i: (0, j))],
        out_shape=[jax.ShapeDtypeStruct((rows, n), out_dtype),
                   jax.ShapeDtypeStruct((rows_s, n), out_dtype)],
        scratch_shapes=[pltpu.VMEM(w_block, BF16)],
        compiler_params=_params("parallel", "arbitrary"),
        name="matmul",
    )(x, xs, w)


def _silu(z):
    return z * _sigmoid(z)


def _forget_gate_kernel(x_ref, w_ref, b_ref, o_ref):
    t = _dot(x_ref[...], w_ref[...]) + b_ref[...]
    o_ref[...] = jnp.minimum(t, 0.0) - jnp.log(1.0 + jnp.exp(-jnp.abs(t)))


def forget_gate(x, w_f, b_f):
    rows, kdim = x.shape
    tm = _row_tile(rows, 1024)
    return pl.pallas_call(
        _forget_gate_kernel,
        grid=(rows // tm,),
        in_specs=[pl.BlockSpec((tm, kdim), lambda i: (i, 0)),
                  pl.BlockSpec((kdim, LANES), lambda i: (0, 0)),
                  pl.BlockSpec((1, LANES), lambda i: (0, 0))],
        out_specs=pl.BlockSpec((tm, LANES), lambda i: (i, 0)),
        out_shape=jax.ShapeDtypeStruct((rows, LANES), F32),
        compiler_params=_params("parallel"),
        name="forget_gate",
    )(x, w_f, b_f)


def _cumsum_kernel(x_ref, o_ref, *, n_chunks):
    r = lax.broadcasted_iota(jnp.int32, (LANES, LANES), 0)
    c = lax.broadcasted_iota(jnp.int32, (LANES, LANES), 1)
    tri = jnp.where(c <= r, 1.0, 0.0).astype(BF16)
    carry = jnp.zeros((1, LANES), F32)
    for n in range(n_chunks):
        hi, mid, lo = _split3(x_ref[0, n * LANES:(n + 1) * LANES, :])
        f = _dot(tri, hi) + _dot(tri, mid) + _dot(tri, lo) + carry
        o_ref[0, n * LANES:(n + 1) * LANES, :] = f
        carry = f[LANES - 1:LANES, :]


def cumsum_rows(x):
    b, length, lanes = x.shape
    return pl.pallas_call(
        functools.partial(_cumsum_kernel, n_chunks=length // LANES),
        grid=(b,),
        in_specs=[pl.BlockSpec((1, length, lanes), lambda i: (i, 0, 0))],
        out_specs=pl.BlockSpec((1, length, lanes), lambda i: (i, 0, 0)),
        out_shape=jax.ShapeDtypeStruct(x.shape, F32),
        compiler_params=_params("parallel"),
        name="cumsum_rows",
    )(x)


def _attn_prompt_kernel(q_ref, k_ref, v_ref, f_ref, fm_ref, frow_ref,
                        sz_ref, o_ref, kb_ref, vt_ref, fcol_ref, s_ref, *, heads_per_step,
                        n_meta, tq):
    hg = pl.program_id(1)
    lq = q_ref.shape[0]
    nq = lq // tq
    n_heads = f_ref.shape[2]
    pad = LANES - n_meta
    key = lax.broadcasted_iota(jnp.int32, (tq, tq), 0)
    qry = lax.broadcasted_iota(jnp.int32, (tq, tq), 1)
    pad_row = lax.broadcasted_iota(jnp.int32, (LANES, 1), 0) < pad

    for hh in range(heads_per_step):
        sl = slice(hh * HEAD_DIM, (hh + 1) * HEAD_DIM)
        head = hg * heads_per_step + hh
        zeros = jnp.zeros((pad, HEAD_DIM), F32)
        k_meta, v_meta = k_ref[0:n_meta, sl], v_ref[0:n_meta, sl]
        kb_ref[hh, 0:LANES, :] = jnp.concatenate([zeros, k_meta], axis=0).astype(BF16)
        kb_ref[hh, LANES:, :] = k_ref[n_meta:, sl].astype(BF16)
        vt_ref[hh, :, 0:LANES] = jnp.concatenate([zeros, v_meta], axis=0).T.astype(BF16)
        for c in range(nq):
            vt_ref[hh, :, LANES + c * tq:LANES + (c + 1) * tq] = (
                v_ref[n_meta + c * tq:n_meta + (c + 1) * tq, sl].T.astype(BF16))
        pick_m = lax.broadcasted_iota(jnp.int32, (LANES, n_heads), 1) == head
        f_meta = jnp.sum(jnp.where(pick_m, fm_ref[0], 0.0), axis=-1, keepdims=True)
        fcol_ref[hh, 0:LANES, :] = jnp.where(pad_row, -MASKED, f_meta)
        pick = lax.broadcasted_iota(jnp.int32, (lq, n_heads), 1) == head
        fcol_ref[hh, LANES:, :] = jnp.sum(jnp.where(pick, f_ref[0], 0.0), axis=-1, keepdims=True)

    for qi in range(nq):
        rows = slice(qi * tq, (qi + 1) * tq)
        tiles = [(0, LANES)] + [(LANES + j * tq, LANES + (j + 1) * tq) for j in range(qi + 1)]
        for hh in range(heads_per_step):
            sl = slice(hh * HEAD_DIM, (hh + 1) * HEAD_DIM)
            buf = (qi * heads_per_step + hh) % s_ref.shape[0]
            q = q_ref[rows, sl]
            f_row = frow_ref[0, hh, qi]
            m = None
            for a, b in tiles:
                s = _dot_nt(kb_ref[hh, a:b, :], q) + (f_row - fcol_ref[hh, a:b, :])
                if b == tiles[-1][1]:
                    s = jnp.where(key <= qry, s, MASKED)
                s_ref[buf, a:b, :] = s
                tile_max = jnp.max(s, axis=0, keepdims=True)
                m = tile_max if m is None else jnp.maximum(m, tile_max)
            l = jnp.zeros((1, tq), F32)
            acc = jnp.zeros((HEAD_DIM, tq), F32)
            for a, b in tiles:
                p = jnp.exp(s_ref[buf, a:b, :] - m)
                l = l + jnp.sum(p, axis=0, keepdims=True)
                acc = acc + _dot(vt_ref[hh, :, a:b], p.astype(BF16))
            gate = sz_ref[rows, sl].astype(F32)
            o_ref[rows, sl] = ((acc / l).T * gate).astype(o_ref.dtype)


def attention_prompt(q, k, v, f_nat, f_meta, f_rows, sz, *, n_prompts, n_meta,
                     heads_per_step=1, tq=256):
    rows, d = q.shape
    lq = rows // n_prompts
    n_heads = d // HEAD_DIM
    w = heads_per_step * HEAD_DIM
    big = pl.BlockSpec((lq, w), lambda b, h: (b, h))
    keys = pl.BlockSpec((n_meta + lq, w), lambda b, h: (b, h))
    return pl.pallas_call(
        functools.partial(_attn_prompt_kernel, heads_per_step=heads_per_step,
                          n_meta=n_meta, tq=tq),
        grid=(n_prompts, n_heads // heads_per_step),
        in_specs=[big, keys, keys,
                  pl.BlockSpec((1, lq, n_heads), lambda b, h: (b, 0, 0)),
                  pl.BlockSpec((1, LANES, n_heads), lambda b, h: (b, 0, 0)),
                  pl.BlockSpec((1, heads_per_step, lq // tq, 1, tq),
                               lambda b, h: (b, h, 0, 0, 0)),
                  big],
        out_specs=big,
        out_shape=jax.ShapeDtypeStruct((rows, d), BF16),
        scratch_shapes=[pltpu.VMEM((heads_per_step, LANES + lq, HEAD_DIM), BF16),
                        pltpu.VMEM((heads_per_step, HEAD_DIM, LANES + lq), BF16),
                        pltpu.VMEM((heads_per_step, LANES + lq, 1), F32),
                        pltpu.VMEM((2, LANES + lq, tq), F32)],
        compiler_params=_params("parallel", "parallel"),
        name="attention_prompt",
    )(q, k, v, f_nat, f_meta, f_rows, sz)


def _attn_decode_kernel(pt_ref, *refs, pages, n_chunks, n_new, n_heads):
    k_refs = refs[0:pages]
    v_refs = refs[pages:2 * pages]
    lf_refs = refs[2 * pages:3 * pages]
    (w_ref, kn_ref, vn_ref, lfn_ref, sz_ref, o_ref,
     kc_ref, vc_ref, acc_ref, m_ref, l_ref, crow_ref, tail_ref, s_ref) = refs[3 * pages:]
    c = pl.program_id(1)
    page = lf_refs[0].shape[1]
    n_pairs = n_heads // 2
    ncol = n_heads * n_new
    w = w_ref[0]

    def to_col(x):
        return jnp.broadcast_to(x, (LANES, ncol)).T[:, 0:1]

    def parity_ok(rows):
        r = lax.broadcasted_iota(jnp.int32, (rows, ncol), 0)
        col = lax.broadcasted_iota(jnp.int32, (rows, ncol), 1)
        return (r & 1) == ((col // n_new) & 1)

    @pl.when(c == 0)
    def _():
        rows_new = lfn_ref.shape[1]
        r = lax.broadcasted_iota(jnp.int32, (rows_new, ncol), 0)
        j = r >> 1
        qcol = lax.broadcasted_iota(jnp.int32, (rows_new, ncol), 1) & (n_new - 1)
        lfn = lfn_ref[0]
        cum = jnp.zeros((rows_new, ncol), F32)
        for jj in range(n_new):
            cum = cum + jnp.where(j >= jj, lfn[2 * jj:2 * jj + 1, :], 0.0)
        crow = jnp.sum(jnp.where(r == 2 * qcol, cum, 0.0), axis=0, keepdims=True)
        s = _dot(kn_ref[0], w) + (crow - cum)
        s = jnp.where(parity_ok(rows_new) & (j <= qcol), s, MASKED)
        m = jnp.max(s, axis=0, keepdims=True)
        p = jnp.exp(s - m)
        m_ref[...] = m
        l_ref[...] = jnp.sum(p, axis=0, keepdims=True)
        acc_ref[...] = _dot(p.T.astype(BF16), vn_ref[0])
        crow_ref[...] = crow
        tail_ref[...] = jnp.zeros_like(tail_ref)
        s_ref[...] = jnp.full(s_ref.shape, MASKED, F32)

    def pair_major(page_ref, dst_ref, p_i):
        words = pltpu.bitcast(page_ref[0].astype(BF16), jnp.uint32)
        words = jnp.swapaxes(words, 0, 1)
        for pr in range(n_pairs):
            dst_ref[2 * p_i * page:2 * (p_i + 1) * page, pr * HEAD_DIM:(pr + 1) * HEAD_DIM] = (
                pltpu.bitcast(words[pr], BF16))

    s_prev = s_ref[...]
    m_old = m_ref[...]
    m_new = jnp.maximum(m_old, jnp.max(s_prev, axis=0, keepdims=True))
    alpha = jnp.exp(m_old - m_new)
    p = jnp.exp(s_prev - m_new)
    m_ref[...] = m_new
    l_ref[...] = alpha * l_ref[...] + jnp.sum(p, axis=0, keepdims=True)
    for p_i in range(pages):
        pair_major(v_refs[p_i], vc_ref, p_i)
    acc_ref[...] = acc_ref[...] * to_col(alpha) + _dot(p.T.astype(BF16), vc_ref[...])

    ri = lax.broadcasted_iota(jnp.int32, (2 * page, page), 0) >> 1
    ci = lax.broadcasted_iota(jnp.int32, (2 * page, page), 1)
    upper = jnp.where(ci > ri, 1.0, 0.0).astype(BF16)
    er = lax.broadcasted_iota(jnp.int32, (LANES, ncol), 0)
    ec = lax.broadcasted_iota(jnp.int32, (LANES, ncol), 1)
    expand = jnp.where(er * n_new == (ec & ~(n_new - 1)), 1.0, 0.0).astype(BF16)
    tail = tail_ref[...]
    decay = [None] * pages
    for p_i in reversed(range(pages)):
        parts = [_dot(x, expand).astype(BF16) for x in _split3(lf_refs[p_i][0])]
        g = tail
        for e in parts:
            g = g + _dot(upper, e)
        decay[p_i] = g
        for e in parts:
            tail = tail + jnp.sum(e.astype(F32), axis=0, keepdims=True)
        pair_major(k_refs[p_i], kc_ref, p_i)
    tail_ref[...] = tail
    s = _dot(kc_ref[...], w) + (crow_ref[...] + jnp.concatenate(decay, axis=0))
    s_ref[...] = jnp.where(parity_ok(2 * pages * page), s, MASKED)

    @pl.when(c == n_chunks)
    def _():
        inv_l = to_col(1.0 / l_ref[...])
        outs = []
        for h in range(n_heads):
            blk = acc_ref[h * n_new:(h + 1) * n_new,
                          (h // 2) * HEAD_DIM:(h // 2 + 1) * HEAD_DIM]
            outs.append(blk * inv_l[h * n_new:(h + 1) * n_new, :])
        o = jnp.concatenate(outs, axis=1)
        o_ref[0] = (o * sz_ref[0].astype(F32)).astype(o_ref.dtype)


def attention_decode(page_table, cache_k, cache_v, cache_lf, w_pairs, k_new, v_new, lf_new, sz,
                     *, n_new, pages=4):
    nb, n_pages = page_table.shape
    _, page, n_heads, _ = cache_k.shape
    d = n_heads * HEAD_DIM
    dp = d // 2
    ncol = n_heads * n_new
    n_chunks = n_pages // pages

    def paged(lag, *tail):
        zeros = (0,) * len(tail)

        def index(p, b, c, pt):
            chunk = jnp.clip(c - lag, 0, n_chunks - 1)
            return (pt[b, (n_chunks - 1 - chunk) * pages + p],) + zeros

        return [pl.BlockSpec((1,) + tail, functools.partial(index, p)) for p in range(pages)]

    per_seq = lambda shape: pl.BlockSpec((1,) + shape, lambda b, c, pt: (b, 0, 0))
    grid_spec = pltpu.PrefetchScalarGridSpec(
        num_scalar_prefetch=1,
        grid=(nb, n_chunks + 1),
        in_specs=paged(0, page, n_heads, HEAD_DIM) + paged(1, page, n_heads, HEAD_DIM)
        + paged(0, page, LANES) + [
            per_seq((dp, ncol)), per_seq((LANES, dp)), per_seq((LANES, dp)),
            per_seq((LANES, ncol)), per_seq((n_new, d))],
        out_specs=per_seq((n_new, d)),
        scratch_shapes=[pltpu.VMEM((2 * pages * page, dp), BF16),
                        pltpu.VMEM((2 * pages * page, dp), BF16),
                        pltpu.VMEM((ncol, dp), F32), pltpu.VMEM((1, ncol), F32),
                        pltpu.VMEM((1, ncol), F32), pltpu.VMEM((1, ncol), F32),
                        pltpu.VMEM((1, ncol), F32), pltpu.VMEM((2 * pages * page, ncol), F32)],
    )
    return pl.pallas_call(
        functools.partial(_attn_decode_kernel, pages=pages, n_chunks=n_chunks, n_new=n_new,
                          n_heads=n_heads),
        grid_spec=grid_spec,
        out_shape=jax.ShapeDtypeStruct((nb, n_new, d), BF16),
        compiler_params=_params("parallel", "arbitrary"),
        name="attention_decode",
    )(page_table, *([cache_k] * pages), *([cache_v] * pages), *([cache_lf] * pages),
      w_pairs, k_new, v_new, lf_new, sz)


def kernel(x_prompt, x_sample, state_conv, cache_k, cache_v, cache_logf, page_table,
           meta_tokens, a_pre_g, a_post_g, a_w_in, a_conv_w, a_w_out, kv_g, w_kvf, b_f,
           b_pre_g, b_post_g, b_w_qz, b_w_o):
    n_prompts, seq, d = x_prompt.shape
    n_dec, n_new, _ = x_sample.shape
    n_meta = meta_tokens.shape[0]
    d_attn = b_w_o.shape[1]
    n_heads = d_attn // HEAD_DIM
    dc = a_conv_w.shape[2]
    n_small = n_meta + n_dec * n_new
    scale = HEAD_DIM ** -0.5

    w_in = a_w_in[0].astype(BF16)
    w_out = a_w_out[0].astype(BF16)
    w_f = jnp.pad(w_kvf[:, 2 * d_attn:], ((0, 0), (0, LANES - n_heads))).astype(BF16)
    bias_f = jnp.pad(b_f, (0, LANES - n_heads)).reshape(1, LANES)
    w_qz = b_w_qz[0]
    w_o = b_w_o[0].astype(BF16)

    x_big = x_prompt.reshape(n_prompts * seq, d)
    x_small = jnp.concatenate([meta_tokens, x_sample.reshape(n_dec * n_new, d)], axis=0)

    starts = jnp.arange(n_small)
    in_dec = starts >= n_meta
    pos = jnp.where(in_dec, (starts - n_meta) % n_new, starts)
    m1 = (pos >= 1).astype(F32).reshape(n_small, 1)
    m2 = (pos >= 2).astype(F32).reshape(n_small, 1)
    st = state_conv[0]
    p1 = jnp.zeros((n_dec, n_new, dc), F32).at[:, 0].set(st[:, 1])
    p2 = jnp.zeros((n_dec, n_new, dc), F32).at[:, 0].set(st[:, 0]).at[:, 1].set(st[:, 1])
    zeros_meta = jnp.zeros((n_meta, dc), F32)
    p1 = jnp.concatenate([zeros_meta, p1.reshape(n_dec * n_new, dc)], axis=0)
    p2 = jnp.concatenate([zeros_meta, p2.reshape(n_dec * n_new, dc)], axis=0)

    (xn_small,) = rms_norm_bf16(x_small, a_pre_g)
    g_small, u_small = conv_mixer_small(xn_small, w_in, a_conv_w[0], m1, m2, p1, p2)
    (xn_big,) = rms_norm_bf16(x_big, a_pre_g)
    ctx = jnp.zeros((CARRY_ROWS, dc), F32).at[CARRY_ROWS - 2:].set(u_small[n_meta - 2:n_meta])
    g_big, tails = conv_mixer_big(xn_big, w_in, a_conv_w[0], ctx, seq)
    h_small = proj_residual(g_small, w_out, x_small, a_post_g)
    h_big = proj_residual(g_big, w_out, x_big, a_post_g)

    state_conv_prompt = tails[None, :, CARRY_ROWS - 2:, :]
    u_dec = u_small[n_meta:].reshape(n_dec, n_new, dc)
    state_conv_sample = u_dec[None, :, n_new - 2:, :]

    gains = jnp.stack([kv_g, b_pre_g[0]], axis=0)
    hkv_small, hb_small = rms_norm_bf16(h_small, gains)
    hkv_big, hb_big = rms_norm_bf16(h_big, gains)

    total = n_meta + seq
    hkv_full = jnp.concatenate(
        [jnp.broadcast_to(hkv_small[None, :n_meta], (n_prompts, n_meta, d)),
         hkv_big.reshape(n_prompts, seq, d)], axis=1).reshape(n_prompts * total, d)
    w_kvf_t = w_kvf.T
    k_full, k_small = matmul(hkv_full, hkv_small, w_kvf_t, 0, d_attn, F32,
                             w_is_transposed=True)
    v_full, v_small = matmul(hkv_full, hkv_small, w_kvf_t, d_attn, d_attn, F32,
                             w_is_transposed=True)
    lf_full = forget_gate(hkv_full, w_f, bias_f)
    lf_small = forget_gate(hkv_small, w_f, bias_f)
    q_big, q_small = matmul(hb_big, hb_small, w_qz, 0, d_attn, BF16,
                            epilogue=lambda y: y * scale)
    sz_big, sz_small = matmul(hb_big, hb_small, w_qz, d_attn, d_attn, BF16, epilogue=_silu)

    lf_prompt = lf_full.reshape(n_prompts, total, LANES)
    padded = -(-total // LANES) * LANES
    f_prompt = cumsum_rows(jnp.pad(lf_prompt, ((0, 0), (0, padded - total), (0, 0))))

    tq = 256
    f_nat = f_prompt[:, n_meta:total, :n_heads]
    f_rows = jnp.transpose(f_nat, (0, 2, 1)).reshape(n_prompts, n_heads, seq // tq, 1, tq)
    f_meta = jnp.pad(f_prompt[:, :n_meta, :n_heads], ((0, 0), (LANES - n_meta, 0), (0, 0)))
    og_big = attention_prompt(q_big, k_full, v_full, f_nat, f_meta, f_rows, sz_big,
                              n_prompts=n_prompts, n_meta=n_meta, tq=tq)
    y_prompt = proj_residual(og_big, w_o, h_big, b_post_g).reshape(n_prompts, seq, d)

    n_rows_dec = n_dec * n_new
    q_dec = q_small[n_meta:].reshape(n_dec, n_new, n_heads, HEAD_DIM)
    n_pairs = n_heads // 2
    ncol = n_heads * n_new
    q_t = jnp.transpose(q_dec, (0, 3, 2, 1)).reshape(n_dec, HEAD_DIM, ncol)
    in_pair = (jnp.arange(n_pairs * HEAD_DIM)[:, None] // HEAD_DIM) == (
        jnp.arange(ncol)[None, :] // (2 * n_new))
    w_pairs = jnp.where(in_pair[None], jnp.tile(q_t, (1, n_pairs, 1)), jnp.zeros((), BF16))

    def pair_rows(x):
        x = x.reshape(n_dec, n_new, n_pairs, 2, HEAD_DIM).transpose(0, 1, 3, 2, 4)
        x = x.reshape(n_dec, 2 * n_new, n_pairs * HEAD_DIM).astype(BF16)
        return jnp.pad(x, ((0, 0), (0, LANES - 2 * n_new), (0, 0)))

    k_new = pair_rows(k_small[n_meta:])
    v_new = pair_rows(v_small[n_meta:])
    lf_dec = lf_small[n_meta:, :n_heads].reshape(n_dec, n_new, n_heads)
    lf_new = jnp.repeat(jnp.repeat(lf_dec, n_new, axis=2), 2, axis=1)
    lf_new = jnp.pad(lf_new, ((0, 0), (0, LANES - 2 * n_new), (0, 0)))
    cache_lf = jnp.pad(cache_logf, ((0, 0), (0, 0), (0, LANES - n_heads)))
    og_dec = attention_decode(
        page_table, cache_k, cache_v, cache_lf, w_pairs, k_new, v_new, lf_new,
        sz_small[n_meta:].reshape(n_dec, n_new, d_attn), n_new=n_new)
    y_sample = proj_residual(og_dec.reshape(n_rows_dec, d_attn), w_o, h_small[n_meta:],
                             b_post_g).reshape(n_dec, n_new, d)

    k_prompt = k_full.reshape(n_prompts, total, n_heads, HEAD_DIM)
    v_prompt = v_full.reshape(n_prompts, total, n_heads, HEAD_DIM)
    logf_prompt = lf_prompt[:, :, :n_heads]
    k_sample = k_small[n_meta:].reshape(n_dec, n_new, n_heads, HEAD_DIM)
    v_sample = v_small[n_meta:].reshape(n_dec, n_new, n_heads, HEAD_DIM)
    logf_sample = lf_dec
    return (y_prompt, y_sample, state_conv_prompt, state_conv_sample,
            k_prompt, v_prompt, logf_prompt, k_sample, v_sample, logf_sample)
```

```python
import functools

import jax
import jax.numpy as jnp
from jax import lax
from jax.experimental import pallas as pl
from jax.experimental.pallas import tpu as pltpu

F32 = jnp.float32
BF16 = jnp.bfloat16

RMS_EPS = 1e-6
CONV_WIDTH = 3
HEAD_DIM = 128
LANES = 128
CARRY_ROWS = 8
MASKED = -1e30
VMEM_LIMIT_BYTES = 60 * 1024 * 1024


def _params(*sem):
    return pltpu.CompilerParams(dimension_semantics=sem, vmem_limit_bytes=VMEM_LIMIT_BYTES)


def _dot(a, b):
    return jnp.dot(a, b, preferred_element_type=F32)


def _dot_nt(a, b):
    return lax.dot_general(a, b, (((1,), (1,)), ((), ())), preferred_element_type=F32)


def _split3(x):
    hi = x.astype(BF16)
    r1 = x - hi.astype(F32)
    mid = r1.astype(BF16)
    lo = (r1 - mid.astype(F32)).astype(BF16)
    return hi, mid, lo


def _sigmoid(z):
    return 1.0 / (1.0 + jnp.exp(-z))


BF16_ROWS = 16


def _row_tile(rows, want):
    for t in range(min(want, rows) // BF16_ROWS * BF16_ROWS, 0, -BF16_ROWS):
        if rows % t == 0:
            return t
    return rows


def _norm_kernel(x_ref, g_ref, *o_refs):
    x = x_ref[...]
    xhat = x * lax.rsqrt(jnp.mean(x * x, axis=-1, keepdims=True) + RMS_EPS)
    for n, o_ref in enumerate(o_refs):
        o_ref[...] = (xhat * g_ref[n:n + 1, :]).astype(o_ref.dtype)


def rms_norm_bf16(x, gains):
    rows, d = x.shape
    n = gains.shape[0]
    tm = _row_tile(rows, 512)
    return pl.pallas_call(
        _norm_kernel,
        grid=(rows // tm,),
        in_specs=[pl.BlockSpec((tm, d), lambda i: (i, 0)),
                  pl.BlockSpec((n, d), lambda i: (0, 0))],
        out_specs=[pl.BlockSpec((tm, d), lambda i: (i, 0))] * n,
        out_shape=[jax.ShapeDtypeStruct((rows, d), BF16)] * n,
        compiler_params=_params("parallel"),
        name="rms_norm",
    )(x, gains)


def _gated_conv(x, wb_ref, wc_ref, wh_ref, wz_ref, cw_ref, um1_fix, um2_fix):
    u = _dot(x, wc_ref[...]) * _dot(x, wh_ref[...])
    um1 = um1_fix(pltpu.roll(u, 1, 0))
    um2 = um2_fix(pltpu.roll(u, 2, 0))
    cw = cw_ref[...]
    y = cw[0:1, :] * um2 + cw[1:2, :] * um1 + cw[2:3, :] * u
    z = _dot(x, wz_ref[...])
    g = _dot(x, wb_ref[...]) * y * (z * _sigmoid(z))
    return u, g


def _conv_big_kernel(x_ref, wb_ref, wc_ref, wh_ref, wz_ref, cw_ref, ctx_ref,
                     g_ref, tail_ref, carry_ref, *, tiles_per_seq):
    i = pl.program_id(1)

    @pl.when(i % tiles_per_seq == 0)
    def _():
        carry_ref[...] = ctx_ref[...]

    carry = carry_ref[...]
    prev1 = carry[CARRY_ROWS - 1:CARRY_ROWS, :]
    prev2 = carry[CARRY_ROWS - 2:CARRY_ROWS - 1, :]
    tm, tn = g_ref.shape
    row = lax.broadcasted_iota(jnp.int32, (tm, tn), 0)
    u, g = _gated_conv(
        x_ref[...], wb_ref, wc_ref, wh_ref, wz_ref, cw_ref,
        lambda r1: jnp.where(row == 0, prev1, r1),
        lambda r2: jnp.where(row == 0, prev2, jnp.where(row == 1, prev1, r2)))
    tail = u[tm - CARRY_ROWS:, :]
    carry_ref[...] = tail
    tail_ref[0] = tail
    g_ref[...] = g.astype(g_ref.dtype)


def _conv_small_kernel(x_ref, wb_ref, wc_ref, wh_ref, wz_ref, cw_ref,
                       m1_ref, m2_ref, p1_ref, p2_ref, g_ref, u_ref):
    u, g = _gated_conv(
        x_ref[...], wb_ref, wc_ref, wh_ref, wz_ref, cw_ref,
        lambda r1: r1 * m1_ref[...] + p1_ref[...],
        lambda r2: r2 * m2_ref[...] + p2_ref[...])
    u_ref[...] = u
    g_ref[...] = g.astype(g_ref.dtype)


def _w_in_specs(d, dc, tn):
    nj = dc // tn
    return [pl.BlockSpec((d, tn), functools.partial(lambda c, j, i: (0, c * nj + j), c))
            for c in range(4)]


def conv_mixer_big(xn, w_in, conv_w, ctx, seq_len, tm=1024, tn=256):
    rows, d = xn.shape
    dc = w_in.shape[1] // 4
    tm = min(tm, seq_len)
    tps = seq_len // tm
    n_seq = rows // seq_len
    return pl.pallas_call(
        functools.partial(_conv_big_kernel, tiles_per_seq=tps),
        grid=(dc // tn, rows // tm),
        in_specs=[pl.BlockSpec((tm, d), lambda j, i: (i, 0))] + _w_in_specs(d, dc, tn) + [
            pl.BlockSpec((CONV_WIDTH, tn), lambda j, i: (0, j)),
            pl.BlockSpec((CARRY_ROWS, tn), lambda j, i: (0, j))],
        out_specs=[pl.BlockSpec((tm, tn), lambda j, i: (i, j)),
                   pl.BlockSpec((1, CARRY_ROWS, tn), lambda j, i: (i // tps, 0, j))],
        out_shape=[jax.ShapeDtypeStruct((rows, dc), BF16),
                   jax.ShapeDtypeStruct((n_seq, CARRY_ROWS, dc), F32)],
        scratch_shapes=[pltpu.VMEM((CARRY_ROWS, tn), F32)],
        compiler_params=_params("arbitrary", "arbitrary"),
        name="conv_mixer_big",
    )(xn, w_in, w_in, w_in, w_in, conv_w, ctx)


def conv_mixer_small(xn, w_in, conv_w, m1, m2, p1, p2, tn=256):
    rows, d = xn.shape
    dc = w_in.shape[1] // 4
    return pl.pallas_call(
        _conv_small_kernel,
        grid=(dc // tn, 1),
        in_specs=[pl.BlockSpec((rows, d), lambda j, i: (0, 0))] + _w_in_specs(d, dc, tn) + [
            pl.BlockSpec((CONV_WIDTH, tn), lambda j, i: (0, j)),
            pl.BlockSpec((rows, 1), lambda j, i: (0, 0)),
            pl.BlockSpec((rows, 1), lambda j, i: (0, 0)),
            pl.BlockSpec((rows, tn), lambda j, i: (0, j)),
            pl.BlockSpec((rows, tn), lambda j, i: (0, j))],
        out_specs=[pl.BlockSpec((rows, tn), lambda j, i: (0, j)),
                   pl.BlockSpec((rows, tn), lambda j, i: (0, j))],
        out_shape=[jax.ShapeDtypeStruct((rows, dc), BF16),
                   jax.ShapeDtypeStruct((rows, dc), F32)],
        compiler_params=_params("parallel", "arbitrary"),
        name="conv_mixer_small",
    )(xn, w_in, w_in, w_in, w_in, conv_w, m1, m2, p1, p2)


def _proj_residual_kernel(x_ref, w_ref, h_ref, g_ref, o_ref):
    a = _dot(x_ref[...], w_ref[...])
    inv = lax.rsqrt(jnp.mean(a * a, axis=-1, keepdims=True) + RMS_EPS)
    o_ref[...] = h_ref[...] + a * inv * g_ref[...]


def proj_residual(x, w, h, gain, tm=256):
    rows, kdim = x.shape
    d = w.shape[1]
    tm = _row_tile(rows, tm)
    return pl.pallas_call(
        _proj_residual_kernel,
        grid=(rows // tm,),
        in_specs=[pl.BlockSpec((tm, kdim), lambda i: (i, 0)),
                  pl.BlockSpec((kdim, d), lambda i: (0, 0), pipeline_mode=pl.Buffered(1)),
                  pl.BlockSpec((tm, d), lambda i: (i, 0)),
                  pl.BlockSpec((1, d), lambda i: (0, 0))],
        out_specs=pl.BlockSpec((tm, d), lambda i: (i, 0)),
        out_shape=jax.ShapeDtypeStruct((rows, d), F32),
        compiler_params=_params("arbitrary"),
        name="proj_residual",
    )(x, w, h, gain)


def _mm_kernel(x_ref, xs_ref, w_ref, o_ref, os_ref, wb_ref, *, epilogue, dot):
    @pl.when(pl.program_id(1) == 0)
    def _():
        wb_ref[...] = w_ref[...].astype(BF16)
        os_ref[...] = epilogue(dot(xs_ref[...], wb_ref[...])).astype(os_ref.dtype)

    o_ref[...] = epilogue(dot(x_ref[...], wb_ref[...])).astype(o_ref.dtype)


def matmul(x, xs, w, col0, n, out_dtype, epilogue=lambda y: y, w_is_transposed=False,
           tm=1024, tn=1024):
    rows, kdim = x.shape
    rows_s = xs.shape[0]
    tm = _row_tile(rows, tm)
    tn = min(tn, n)
    j0 = col0 // tn
    if w_is_transposed:
        w_block, w_index, dot = (tn, kdim), (lambda j, i: (j0 + j, 0)), _dot_nt
    else:
        w_block, w_index, dot = (kdim, tn), (lambda j, i: (0, j0 + j)), _dot
    w_spec = pl.BlockSpec(w_block, w_index, pipeline_mode=pl.Buffered(1))
    return pl.pallas_call(
        functools.partial(_mm_kernel, epilogue=epilogue, dot=dot),
        grid=(n // tn, rows // tm),
        in_specs=[pl.BlockSpec((tm, kdim), lambda j, i: (i, 0)),
                  pl.BlockSpec((rows_s, kdim), lambda j, i: (0, 0)),
                  w_spec],
        out_specs=[pl.BlockSpec((tm, tn), lambda j, i: (i, j)),
                   pl.BlockSpec((rows_s, tn), lambda j, i: (0, j))],
        out_shape=[jax.ShapeDtypeStruct((rows, n), out_dtype),
                   jax.ShapeDtypeStruct((rows_s, n), out_dtype)],
        scratch_shapes=[pltpu.VMEM(w_block, BF16)],
        compiler_params=_params("parallel", "arbitrary"),
        name="matmul",
    )(x, xs, w)


def _silu(z):
    return z * _sigmoid(z)


def _forget_gate_kernel(x_ref, w_ref, b_ref, o_ref):
    t = _dot(x_ref[...], w_ref[...]) + b_ref[...]
    o_ref[...] = jnp.minimum(t, 0.0) - jnp.log(1.0 + jnp.exp(-jnp.abs(t)))


def forget_gate(x, w_f, b_f):
    rows, kdim = x.shape
    tm = _row_tile(rows, 1024)
    return pl.pallas_call(
        _forget_gate_kernel,
        grid=(rows // tm,),
        in_specs=[pl.BlockSpec((tm, kdim), lambda i: (i, 0)),
                  pl.BlockSpec((kdim, LANES), lambda i: (0, 0)),
                  pl.BlockSpec((1, LANES), lambda i: (0, 0))],
        out_specs=pl.BlockSpec((tm, LANES), lambda i: (i, 0)),
        out_shape=jax.ShapeDtypeStruct((rows, LANES), F32),
        compiler_params=_params("parallel"),
        name="forget_gate",
    )(x, w_f, b_f)


def _cumsum_kernel(x_ref, o_ref, *, n_chunks):
    r = lax.broadcasted_iota(jnp.int32, (LANES, LANES), 0)
    c = lax.broadcasted_iota(jnp.int32, (LANES, LANES), 1)
    tri = jnp.where(c <= r, 1.0, 0.0).astype(BF16)
    carry = jnp.zeros((1, LANES), F32)
    for n in range(n_chunks):
        hi, mid, lo = _split3(x_ref[0, n * LANES:(n + 1) * LANES, :])
        f = _dot(tri, hi) + _dot(tri, mid) + _dot(tri, lo) + carry
        o_ref[0, n * LANES:(n + 1) * LANES, :] = f
        carry = f[LANES - 1:LANES, :]


def cumsum_rows(x):
    b, length, lanes = x.shape
    return pl.pallas_call(
        functools.partial(_cumsum_kernel, n_chunks=length // LANES),
        grid=(b,),
        in_specs=[pl.BlockSpec((1, length, lanes), lambda i: (i, 0, 0))],
        out_specs=pl.BlockSpec((1, length, lanes), lambda i: (i, 0, 0)),
        out_shape=jax.ShapeDtypeStruct(x.shape, F32),
        compiler_params=_params("parallel"),
        name="cumsum_rows",
    )(x)


def _attn_prompt_kernel(q_ref, k_ref, v_ref, f_ref, fm_ref, frow_ref,
                        sz_ref, o_ref, kb_ref, vt_ref, fcol_ref, s_ref, *, heads_per_step,
                        n_meta, tq):
    hg = pl.program_id(1)
    lq = q_ref.shape[0]
    nq = lq // tq
    n_heads = f_ref.shape[2]
    pad = LANES - n_meta
    key = lax.broadcasted_iota(jnp.int32, (tq, tq), 0)
    qry = lax.broadcasted_iota(jnp.int32, (tq, tq), 1)
    pad_row = lax.broadcasted_iota(jnp.int32, (LANES, 1), 0) < pad

    for hh in range(heads_per_step):
        sl = slice(hh * HEAD_DIM, (hh + 1) * HEAD_DIM)
        head = hg * heads_per_step + hh
        zeros = jnp.zeros((pad, HEAD_DIM), F32)
        k_meta, v_meta = k_ref[0:n_meta, sl], v_ref[0:n_meta, sl]
        kb_ref[hh, 0:LANES, :] = jnp.concatenate([zeros, k_meta], axis=0).astype(BF16)
        kb_ref[hh, LANES:, :] = k_ref[n_meta:, sl].astype(BF16)
        vt_ref[hh, :, 0:LANES] = jnp.concatenate([zeros, v_meta], axis=0).T.astype(BF16)
        for c in range(nq):
            vt_ref[hh, :, LANES + c * tq:LANES + (c + 1) * tq] = (
                v_ref[n_meta + c * tq:n_meta + (c + 1) * tq, sl].T.astype(BF16))
        pick_m = lax.broadcasted_iota(jnp.int32, (LANES, n_heads), 1) == head
        f_meta = jnp.sum(jnp.where(pick_m, fm_ref[0], 0.0), axis=-1, keepdims=True)
        fcol_ref[hh, 0:LANES, :] = jnp.where(pad_row, -MASKED, f_meta)
        pick = lax.broadcasted_iota(jnp.int32, (lq, n_heads), 1) == head
        fcol_ref[hh, LANES:, :] = jnp.sum(jnp.where(pick, f_ref[0], 0.0), axis=-1, keepdims=True)

    for qi in range(nq):
        rows = slice(qi * tq, (qi + 1) * tq)
        tiles = [(0, LANES)] + [(LANES + j * tq, LANES + (j + 1) * tq) for j in range(qi + 1)]
        for hh in range(heads_per_step):
            sl = slice(hh * HEAD_DIM, (hh + 1) * HEAD_DIM)
            buf = (qi * heads_per_step + hh) % s_ref.shape[0]
            q = q_ref[rows, sl]
            f_row = frow_ref[0, hh, qi]
            m = None
            for a, b in tiles:
                s = _dot_nt(kb_ref[hh, a:b, :], q) + (f_row - fcol_ref[hh, a:b, :])
                if b == tiles[-1][1]:
                    s = jnp.where(key <= qry, s, MASKED)
                s_ref[buf, a:b, :] = s
                tile_max = jnp.max(s, axis=0, keepdims=True)
                m = tile_max if m is None else jnp.maximum(m, tile_max)
            l = jnp.zeros((1, tq), F32)
            acc = jnp.zeros((HEAD_DIM, tq), F32)
            for a, b in tiles:
                p = jnp.exp(s_ref[buf, a:b, :] - m)
                l = l + jnp.sum(p, axis=0, keepdims=True)
                acc = acc + _dot(vt_ref[hh, :, a:b], p.astype(BF16))
            gate = sz_ref[rows, sl].astype(F32)
            o_ref[rows, sl] = ((acc / l).T * gate).astype(o_ref.dtype)


def attention_prompt(q, k, v, f_nat, f_meta, f_rows, sz, *, n_prompts, n_meta,
                     heads_per_step=1, tq=256):
    rows, d = q.shape
    lq = rows // n_prompts
    n_heads = d // HEAD_DIM
    w = heads_per_step * HEAD_DIM
    big = pl.BlockSpec((lq, w), lambda b, h: (b, h))
    keys = pl.BlockSpec((n_meta + lq, w), lambda b, h: (b, h))
    return pl.pallas_call(
        functools.partial(_attn_prompt_kernel, heads_per_step=heads_per_step,
                          n_meta=n_meta, tq=tq),
        grid=(n_prompts, n_heads // heads_per_step),
        in_specs=[big, keys, keys,
                  pl.BlockSpec((1, lq, n_heads), lambda b, h: (b, 0, 0)),
                  pl.BlockSpec((1, LANES, n_heads), lambda b, h: (b, 0, 0)),
                  pl.BlockSpec((1, heads_per_step, lq // tq, 1, tq),
                               lambda b, h: (b, h, 0, 0, 0)),
                  big],
        out_specs=big,
        out_shape=jax.ShapeDtypeStruct((rows, d), BF16),
        scratch_shapes=[pltpu.VMEM((heads_per_step, LANES + lq, HEAD_DIM), BF16),
                        pltpu.VMEM((heads_per_step, HEAD_DIM, LANES + lq), BF16),
                        pltpu.VMEM((heads_per_step, LANES + lq, 1), F32),
                        pltpu.VMEM((2, LANES + lq, tq), F32)],
        compiler_params=_params("parallel", "parallel"),
        name="attention_prompt",
    )(q, k, v, f_nat, f_meta, f_rows, sz)


def _attn_decode_kernel(pt_ref, *refs, pages, n_chunks, n_new, n_heads):
    k_refs = refs[0:pages]
    v_refs = refs[pages:2 * pages]
    lf_refs = refs[2 * pages:3 * pages]
    (w_ref, kn_ref, vn_ref, lfn_ref, sz_ref, o_ref,
     kc_ref, vc_ref, acc_ref, m_ref, l_ref, crow_ref, tail_ref, s_ref) = refs[3 * pages:]
    c = pl.program_id(1)
    page = lf_refs[0].shape[1]
    n_pairs = n_heads // 2
    ncol = n_heads * n_new
    w = w_ref[0]

    def to_col(x):
        return jnp.broadcast_to(x, (LANES, ncol)).T[:, 0:1]

    def parity_ok(rows):
        r = lax.broadcasted_iota(jnp.int32, (rows, ncol), 0)
        col = lax.broadcasted_iota(jnp.int32, (rows, ncol), 1)
        return (r & 1) == ((col // n_new) & 1)

    @pl.when(c == 0)
    def _():
        rows_new = lfn_ref.shape[1]
        r = lax.broadcasted_iota(jnp.int32, (rows_new, ncol), 0)
        j = r >> 1
        qcol = lax.broadcasted_iota(jnp.int32, (rows_new, ncol), 1) & (n_new - 1)
        lfn = lfn_ref[0]
        cum = jnp.zeros((rows_new, ncol), F32)
        for jj in range(n_new):
            cum = cum + jnp.where(j >= jj, lfn[2 * jj:2 * jj + 1, :], 0.0)
        crow = jnp.sum(jnp.where(r == 2 * qcol, cum, 0.0), axis=0, keepdims=True)
        s = _dot(kn_ref[0], w) + (crow - cum)
        s = jnp.where(parity_ok(rows_new) & (j <= qcol), s, MASKED)
        m = jnp.max(s, axis=0, keepdims=True)
        p = jnp.exp(s - m)
        m_ref[...] = m
        l_ref[...] = jnp.sum(p, axis=0, keepdims=True)
        acc_ref[...] = _dot(p.T.astype(BF16), vn_ref[0])
        crow_ref[...] = crow
        tail_ref[...] = jnp.zeros_like(tail_ref)
        s_ref[...] = jnp.full(s_ref.shape, MASKED, F32)

    def pair_major(page_ref, dst_ref, p_i):
        words = pltpu.bitcast(page_ref[0].astype(BF16), jnp.uint32)
        words = jnp.swapaxes(words, 0, 1)
        for pr in range(n_pairs):
            dst_ref[2 * p_i * page:2 * (p_i + 1) * page, pr * HEAD_DIM:(pr + 1) * HEAD_DIM] = (
                pltpu.bitcast(words[pr], BF16))

    s_prev = s_ref[...]
    m_old = m_ref[...]
    m_new = jnp.maximum(m_old, jnp.max(s_prev, axis=0, keepdims=True))
    alpha = jnp.exp(m_old - m_new)
    p = jnp.exp(s_prev - m_new)
    m_ref[...] = m_new
    l_ref[...] = alpha * l_ref[...] + jnp.sum(p, axis=0, keepdims=True)
    for p_i in range(pages):
        pair_major(v_refs[p_i], vc_ref, p_i)
    acc_ref[...] = acc_ref[...] * to_col(alpha) + _dot(p.T.astype(BF16), vc_ref[...])

    ri = lax.broadcasted_iota(jnp.int32, (2 * page, page), 0) >> 1
    ci = lax.broadcasted_iota(jnp.int32, (2 * page, page), 1)
    upper = jnp.where(ci > ri, 1.0, 0.0).astype(BF16)
    er = lax.broadcasted_iota(jnp.int32, (n_heads, ncol), 0)
    ec = lax.broadcasted_iota(jnp.int32, (n_heads, ncol), 1)
    expand = jnp.where(er * n_new == (ec & ~(n_new - 1)), 1.0, 0.0).astype(BF16)
    tail = tail_ref[...]
    decay = [None] * pages
    for p_i in reversed(range(pages)):
        parts = [_dot(x, expand).astype(BF16) for x in _split3(lf_refs[p_i][0])]
        g = tail
        for e in parts:
            g = g + _dot(upper, e)
        decay[p_i] = g
        for e in parts:
            tail = tail + jnp.sum(e.astype(F32), axis=0, keepdims=True)
        pair_major(k_refs[p_i], kc_ref, p_i)
    tail_ref[...] = tail
    s = _dot(kc_ref[...], w) + (crow_ref[...] + jnp.concatenate(decay, axis=0))
    s_ref[...] = jnp.where(parity_ok(2 * pages * page), s, MASKED)

    @pl.when(c == n_chunks)
    def _():
        inv_l = to_col(1.0 / l_ref[...])
        outs = []
        for h in range(n_heads):
            blk = acc_ref[h * n_new:(h + 1) * n_new,
                          (h // 2) * HEAD_DIM:(h // 2 + 1) * HEAD_DIM]
            outs.append(blk * inv_l[h * n_new:(h + 1) * n_new, :])
        o = jnp.concatenate(outs, axis=1)
        o_ref[0] = (o * sz_ref[0].astype(F32)).astype(o_ref.dtype)


def attention_decode(page_table, cache_k, cache_v, cache_lf, w_pairs, k_new, v_new, lf_new, sz,
                     *, n_new, pages=4):
    nb, n_pages = page_table.shape
    _, page, n_heads, _ = cache_k.shape
    d = n_heads * HEAD_DIM
    dp = d // 2
    ncol = n_heads * n_new
    n_chunks = n_pages // pages

    def paged(lag, *tail):
        zeros = (0,) * len(tail)

        def index(p, b, c, pt):
            chunk = jnp.clip(c - lag, 0, n_chunks - 1)
            return (pt[b, (n_chunks - 1 - chunk) * pages + p],) + zeros

        return [pl.BlockSpec((1,) + tail, functools.partial(index, p)) for p in range(pages)]

    per_seq = lambda shape: pl.BlockSpec((1,) + shape, lambda b, c, pt: (b, 0, 0))
    grid_spec = pltpu.PrefetchScalarGridSpec(
        num_scalar_prefetch=1,
        grid=(nb, n_chunks + 1),
        in_specs=paged(0, page, n_heads, HEAD_DIM) + paged(1, page, n_heads, HEAD_DIM)
        + paged(0, page, n_heads) + [
            per_seq((dp, ncol)), per_seq((LANES, dp)), per_seq((LANES, dp)),
            per_seq((LANES, ncol)), per_seq((n_new, d))],
        out_specs=per_seq((n_new, d)),
        scratch_shapes=[pltpu.VMEM((2 * pages * page, dp), BF16),
                        pltpu.VMEM((2 * pages * page, dp), BF16),
                        pltpu.VMEM((ncol, dp), F32), pltpu.VMEM((1, ncol), F32),
                        pltpu.VMEM((1, ncol), F32), pltpu.VMEM((1, ncol), F32),
                        pltpu.VMEM((1, ncol), F32), pltpu.VMEM((2 * pages * page, ncol), F32)],
    )
    return pl.pallas_call(
        functools.partial(_attn_decode_kernel, pages=pages, n_chunks=n_chunks, n_new=n_new,
                          n_heads=n_heads),
        grid_spec=grid_spec,
        out_shape=jax.ShapeDtypeStruct((nb, n_new, d), BF16),
        compiler_params=_params("parallel", "arbitrary"),
        name="attention_decode",
    )(page_table, *([cache_k] * pages), *([cache_v] * pages), *([cache_lf] * pages),
      w_pairs, k_new, v_new, lf_new, sz)


def kernel(x_prompt, x_sample, state_conv, cache_k, cache_v, cache_logf, page_table,
           meta_tokens, a_pre_g, a_post_g, a_w_in, a_conv_w, a_w_out, kv_g, w_kvf, b_f,
           b_pre_g, b_post_g, b_w_qz, b_w_o):
    n_prompts, seq, d = x_prompt.shape
    n_dec, n_new, _ = x_sample.shape
    n_meta = meta_tokens.shape[0]
    d_attn = b_w_o.shape[1]
    n_heads = d_attn // HEAD_DIM
    dc = a_conv_w.shape[2]
    n_small = n_meta + n_dec * n_new
    scale = HEAD_DIM ** -0.5

    w_in = a_w_in[0].astype(BF16)
    w_out = a_w_out[0].astype(BF16)
    w_f = jnp.pad(w_kvf[:, 2 * d_attn:], ((0, 0), (0, LANES - n_heads))).astype(BF16)
    bias_f = jnp.pad(b_f, (0, LANES - n_heads)).reshape(1, LANES)
    w_qz = b_w_qz[0]
    w_o = b_w_o[0].astype(BF16)

    x_big = x_prompt.reshape(n_prompts * seq, d)
    x_small = jnp.concatenate([meta_tokens, x_sample.reshape(n_dec * n_new, d)], axis=0)

    starts = jnp.arange(n_small)
    in_dec = starts >= n_meta
    pos = jnp.where(in_dec, (starts - n_meta) % n_new, starts)
    m1 = (pos >= 1).astype(F32).reshape(n_small, 1)
    m2 = (pos >= 2).astype(F32).reshape(n_small, 1)
    st = state_conv[0]
    p1 = jnp.zeros((n_dec, n_new, dc), F32).at[:, 0].set(st[:, 1])
    p2 = jnp.zeros((n_dec, n_new, dc), F32).at[:, 0].set(st[:, 0]).at[:, 1].set(st[:, 1])
    zeros_meta = jnp.zeros((n_meta, dc), F32)
    p1 = jnp.concatenate([zeros_meta, p1.reshape(n_dec * n_new, dc)], axis=0)
    p2 = jnp.concatenate([zeros_meta, p2.reshape(n_dec * n_new, dc)], axis=0)

    (xn_small,) = rms_norm_bf16(x_small, a_pre_g)
    g_small, u_small = conv_mixer_small(xn_small, w_in, a_conv_w[0], m1, m2, p1, p2)
    (xn_big,) = rms_norm_bf16(x_big, a_pre_g)
    ctx = jnp.zeros((CARRY_ROWS, dc), F32).at[CARRY_ROWS - 2:].set(u_small[n_meta - 2:n_meta])
    g_big, tails = conv_mixer_big(xn_big, w_in, a_conv_w[0], ctx, seq)
    h_small = proj_residual(g_small, w_out, x_small, a_post_g)
    h_big = proj_residual(g_big, w_out, x_big, a_post_g)

    state_conv_prompt = tails[None, :, CARRY_ROWS - 2:, :]
    u_dec = u_small[n_meta:].reshape(n_dec, n_new, dc)
    state_conv_sample = u_dec[None, :, n_new - 2:, :]

    gains = jnp.stack([kv_g, b_pre_g[0]], axis=0)
    hkv_small, hb_small = rms_norm_bf16(h_small, gains)
    hkv_big, hb_big = rms_norm_bf16(h_big, gains)

    total = n_meta + seq
    hkv_full = jnp.concatenate(
        [jnp.broadcast_to(hkv_small[None, :n_meta], (n_prompts, n_meta, d)),
         hkv_big.reshape(n_prompts, seq, d)], axis=1).reshape(n_prompts * total, d)
    w_kvf_t = w_kvf.T
    k_full, k_small = matmul(hkv_full, hkv_small, w_kvf_t, 0, d_attn, F32,
                             w_is_transposed=True)
    v_full, v_small = matmul(hkv_full, hkv_small, w_kvf_t, d_attn, d_attn, F32,
                             w_is_transposed=True)
    lf_full = forget_gate(hkv_full, w_f, bias_f)
    lf_small = forget_gate(hkv_small, w_f, bias_f)
    q_big, q_small = matmul(hb_big, hb_small, w_qz, 0, d_attn, BF16,
                            epilogue=lambda y: y * scale)
    sz_big, sz_small = matmul(hb_big, hb_small, w_qz, d_attn, d_attn, BF16, epilogue=_silu)

    lf_prompt = lf_full.reshape(n_prompts, total, LANES)
    padded = -(-total // LANES) * LANES
    f_prompt = cumsum_rows(jnp.pad(lf_prompt, ((0, 0), (0, padded - total), (0, 0))))

    tq = 256
    f_nat = f_prompt[:, n_meta:total, :n_heads]
    f_rows = jnp.transpose(f_nat, (0, 2, 1)).reshape(n_prompts, n_heads, seq // tq, 1, tq)
    f_meta = jnp.pad(f_prompt[:, :n_meta, :n_heads], ((0, 0), (LANES - n_meta, 0), (0, 0)))
    og_big = attention_prompt(q_big, k_full, v_full, f_nat, f_meta, f_rows, sz_big,
                              n_prompts=n_prompts, n_meta=n_meta, tq=tq)
    y_prompt = proj_residual(og_big, w_o, h_big, b_post_g).reshape(n_prompts, seq, d)

    n_rows_dec = n_dec * n_new
    q_dec = q_small[n_meta:].reshape(n_dec, n_new, n_heads, HEAD_DIM)
    n_pairs = n_heads // 2
    ncol = n_heads * n_new
    q_t = jnp.transpose(q_dec, (0, 3, 2, 1)).reshape(n_dec, HEAD_DIM, ncol)
    in_pair = (jnp.arange(n_pairs * HEAD_DIM)[:, None] // HEAD_DIM) == (
        jnp.arange(ncol)[None, :] // (2 * n_new))
    w_pairs = jnp.where(in_pair[None], jnp.tile(q_t, (1, n_pairs, 1)), jnp.zeros((), BF16))

    def pair_rows(x):
        x = x.reshape(n_dec, n_new, n_pairs, 2, HEAD_DIM).transpose(0, 1, 3, 2, 4)
        x = x.reshape(n_dec, 2 * n_new, n_pairs * HEAD_DIM).astype(BF16)
        return jnp.pad(x, ((0, 0), (0, LANES - 2 * n_new), (0, 0)))

    k_new = pair_rows(k_small[n_meta:])
    v_new = pair_rows(v_small[n_meta:])
    lf_dec = lf_small[n_meta:, :n_heads].reshape(n_dec, n_new, n_heads)
    lf_new = jnp.repeat(jnp.repeat(lf_dec, n_new, axis=2), 2, axis=1)
    lf_new = jnp.pad(lf_new, ((0, 0), (0, LANES - 2 * n_new), (0, 0)))
    og_dec = attention_decode(
        page_table, cache_k, cache_v, cache_logf, w_pairs, k_new, v_new, lf_new,
        sz_small[n_meta:].reshape(n_dec, n_new, d_attn), n_new=n_new)
    y_sample = proj_residual(og_dec.reshape(n_rows_dec, d_attn), w_o, h_small[n_meta:],
                             b_post_g).reshape(n_dec, n_new, d)

    k_prompt = k_full.reshape(n_prompts, total, n_heads, HEAD_DIM)
    v_prompt = v_full.reshape(n_prompts, total, n_heads, HEAD_DIM)
    logf_prompt = lf_prompt[:, :, :n_heads]
    k_sample = k_small[n_meta:].reshape(n_dec, n_new, n_heads, HEAD_DIM)
    v_sample = v_small[n_meta:].reshape(n_dec, n_new, n_heads, HEAD_DIM)
    logf_sample = lf_dec
    return (y_prompt, y_sample, state_conv_prompt, state_conv_sample,
            k_prompt, v_prompt, logf_prompt, k_sample, v_sample, logf_sample)
```
